```python
import math
import jax
import jax.numpy as jnp
from jax import lax
import numpy as np

D_MODEL = 1024
BATCH = 2
SEQ = 8192
DEPTH = 4
DEC_BATCH = 32
DEC_SEQ = 8
PAST_LEN = 8192
PAGE_SIZE = 128

MIX_WIDTH = D_MODEL // 2
HEAD_DIM = 64
N_BRANCH = 3
ATT_HEADS = MIX_WIDTH // HEAD_DIM
ATT_WIDTH = ATT_HEADS * HEAD_DIM
MOBA_BLOCK = 256
MOBA_TOPK = 3
Q_BLOCK = 128
ROPE_THETA = 10000.0
SSM_HEAD_DIM = 64
SSM_INNER = MIX_WIDTH
SSM_HEADS = SSM_INNER // SSM_HEAD_DIM
SSM_GROUPS = 2
SSM_STATE = 128
SSM_CONV = 4
SSM_CHUNK = 128
SSM_CONV_DIM = SSM_INNER + 2 * SSM_GROUPS * SSM_STATE
RWKV_WIDTH = MIX_WIDTH
RWKV_HEADS = RWKV_WIDTH // HEAD_DIM
DECAY_LORA = 64
ICLR_LORA = 64
GATE_LORA = 128
RWKV_SHIFT_DIM = 3 * RWKV_WIDTH + DECAY_LORA + ICLR_LORA + GATE_LORA
RWKV_GN_EPS = 64e-5
D_FF = ((8 * D_MODEL // 3 + 127) // 128) * 128
FFN_CONV = 3
EPS = 1e-6
IN_SPLITS = [N_BRANCH * D_MODEL, ATT_WIDTH, ATT_WIDTH, ATT_WIDTH,
             SSM_INNER, SSM_CONV_DIM, SSM_HEADS, RWKV_SHIFT_DIM]
IN_WIDTH = sum(IN_SPLITS)
RWKV_SPLITS = [RWKV_WIDTH, RWKV_WIDTH, RWKV_WIDTH, DECAY_LORA, ICLR_LORA, GATE_LORA]

kernel_name = 'hybrid_moba_ssd_rwkv7_convffn_step'


def split_cols(t, sizes):
    idx = np.cumsum(sizes)[:-1].tolist()
    return jnp.split(t, idx, axis=-1)


def rms_norm(x, g):
    xf = x.astype(jnp.float32)
    y = xf * lax.rsqrt(jnp.mean(xf * xf, axis=-1, keepdims=True) + EPS)
    return (y * g).astype(x.dtype)


def causal_dwconv(u, prev, w, b):
    K = w.shape[0]
    T = u.shape[1]
    full = jnp.concatenate([prev.astype(u.dtype), u], axis=1)
    out = b + sum(w[i] * full[:, i:i + T] for i in range(K))
    return out, full[:, T:]


def rope(x, pos):
    half = x.shape[-1] // 2
    inv = ROPE_THETA ** (-jnp.arange(half, dtype=jnp.float32) / half)
    ang = pos.astype(jnp.float32)[:, None] * inv[None, :]
    cos = jnp.cos(ang)[None, :, None, :]
    sin = jnp.sin(ang)[None, :, None, :]
    xf = x.astype(jnp.float32)
    x1, x2 = xf[..., :half], xf[..., half:]
    return jnp.concatenate([x1 * cos - x2 * sin, x2 * cos + x1 * sin], axis=-1).astype(x.dtype)


def moba_attention(q, k_past, k_new, v_past, v_new, pos0):
    n, T, H, d = q.shape
    L = k_past.shape[1] + T
    nb = -(-L // MOBA_BLOCK)
    pad = jnp.zeros((n, nb * MOBA_BLOCK - L, H, d), q.dtype)
    k_all = jnp.concatenate([k_past.astype(q.dtype), k_new, pad], axis=1)
    v_all = jnp.concatenate([v_past.astype(q.dtype), v_new, pad], axis=1)
    kb = k_all.reshape(n, nb, MOBA_BLOCK, H, d)
    vb = v_all.reshape(n, nb, MOBA_BLOCK, H, d)
    k_mean = jnp.mean(kb.astype(jnp.float32), axis=2).astype(q.dtype)
    n_sel = min(MOBA_TOPK, nb)
    qc = min(Q_BLOCK, T)
    nc = T // qc
    q_blocks = jnp.moveaxis(q.reshape(n, nc, qc, H, d), 1, 0)
    starts = pos0 + qc * jnp.arange(nc)
    bi = jnp.arange(n)[:, None, None, None]
    hi = jnp.arange(H)[None, :, None, None]
    scale = d ** -0.5

    def one_block(args):
        qb, s0 = args
        q_pos = s0 + jnp.arange(qc)
        j = s0 // MOBA_BLOCK
        gate = jnp.einsum('bqhd,bnhd->bhqn', qb, k_mean).astype(jnp.float32)
        gate = jnp.where(jnp.arange(nb) < j, gate, -jnp.inf)
        _, idx = lax.top_k(gate, n_sel)
        k_sel = kb[bi, idx, :, hi]
        v_sel = vb[bi, idx, :, hi]
        own0 = j * MOBA_BLOCK
        k_own = lax.dynamic_slice_in_dim(k_all, own0, MOBA_BLOCK, axis=1)
        v_own = lax.dynamic_slice_in_dim(v_all, own0, MOBA_BLOCK, axis=1)
        s_sel = jnp.einsum('bqhd,bhqnkd->bhqnk', qb, k_sel).astype(jnp.float32) * scale
        s_sel = jnp.where((jnp.arange(n_sel) < j)[:, None], s_sel, -jnp.inf)
        s_own = jnp.einsum('bqhd,bkhd->bhqk', qb, k_own).astype(jnp.float32) * scale
        s_own = jnp.where((own0 + jnp.arange(MOBA_BLOCK))[None, :] <= q_pos[:, None], s_own, -jnp.inf)
        s = jnp.concatenate([s_sel.reshape(n, H, qc, n_sel * MOBA_BLOCK), s_own], axis=-1)
        p = jax.nn.softmax(s, axis=-1).astype(q.dtype)
        p_sel = p[..., :n_sel * MOBA_BLOCK].reshape(n, H, qc, n_sel, MOBA_BLOCK)
        p_own = p[..., n_sel * MOBA_BLOCK:]
        return (jnp.einsum('bhqnk,bhqnkd->bqhd', p_sel, v_sel)
                + jnp.einsum('bhqk,bkhd->bqhd', p_own, v_own))

    out = lax.map(one_block, (q_blocks, starts))
    return jnp.moveaxis(out, 0, 1).reshape(n, T, H, d)


def ssd_scan(x, dt, A, Bm, Cm, S0):
    n, T, H, P = x.shape
    G, N = Bm.shape[2], Bm.shape[3]
    R = H // G
    Q = math.gcd(T, SSM_CHUNK)
    nc = T // Q
    xc = x.reshape(n, nc, Q, G, R, P)
    dtc = dt.reshape(n, nc, Q, G, R)
    Bc = Bm.reshape(n, nc, Q, G, N)
    Cc = Cm.reshape(n, nc, Q, G, N)
    cum = jnp.cumsum(dtc * A.reshape(G, R), axis=2)
    cum_t = jnp.moveaxis(cum, 2, -1)
    causal = jnp.tril(jnp.ones((Q, Q), dtype=bool))
    seg = cum_t[..., :, None] - cum_t[..., None, :]
    Lmat = jnp.exp(jnp.where(causal, seg, -jnp.inf))
    CB = jnp.einsum('bcign,bcjgn->bcgij', Cc, Bc)
    xdt = xc * dtc[..., None]
    y_diag = jnp.einsum('bcgrij,bcjgrp->bcigrp', CB[:, :, :, None] * Lmat, xdt)
    decay_end = jnp.exp(cum_t[..., -1:] - cum_t)
    S_chunk = jnp.einsum('bcjgn,bcgrj,bcjgrp->bcgrnp', Bc, decay_end, xdt)
    chunk_decay = jnp.exp(cum_t[..., -1])

    def pass_state(S, inp):
        dec, Sc = inp
        return S * dec[..., None, None] + Sc, S

    S_fin, S_in = lax.scan(pass_state, S0.reshape(n, G, R, N, P).astype(S_chunk.dtype),
                           (jnp.moveaxis(chunk_decay, 1, 0), jnp.moveaxis(S_chunk, 1, 0)))
    S_in = jnp.moveaxis(S_in, 0, 1)
    y_off = jnp.einsum('bcign,bcgri,bcgrnp->bcigrp', Cc, jnp.exp(cum_t), S_in)
    y = (y_diag + y_off).reshape(n, T, H, P).astype(x.dtype)
    return y, S_fin.reshape(n, H, N, P)


def ssm_branch(z, xbc, dt_raw, conv_prev, S0, conv_w, conv_b, dt_bias, A_log, D_skip, norm_g):
    n, T, _ = z.shape
    xbc_c, conv_new = causal_dwconv(xbc, conv_prev, conv_w, conv_b)
    xbc_c = jax.nn.silu(xbc_c)
    xs, Bm, Cm = split_cols(xbc_c, [SSM_INNER, SSM_GROUPS * SSM_STATE, SSM_GROUPS * SSM_STATE])
    xs = xs.reshape(n, T, SSM_HEADS, SSM_HEAD_DIM)
    Bm = Bm.reshape(n, T, SSM_GROUPS, SSM_STATE)
    Cm = Cm.reshape(n, T, SSM_GROUPS, SSM_STATE)
    dt = jax.nn.softplus(dt_raw + dt_bias)
    A = -jnp.exp(A_log)
    y, S_new = ssd_scan(xs, dt, A, Bm, Cm, S0)
    y = (y + D_skip[:, None] * xs).reshape(n, T, SSM_INNER) * jax.nn.silu(z)
    y = rms_norm(y.reshape(n, T, SSM_GROUPS, -1), norm_g.reshape(SSM_GROUPS, -1)).reshape(n, T, SSM_INNER)
    return y, S_new, conv_new


def rwkv_recurrence(r, decay, k, v, kk, a, S0):
    dt_ = r.dtype

    def step(S, inp):
        r_t, w_t, k_t, v_t, kk_t, a_t = inp
        sa = jnp.einsum('bhvk,bhk->bhv', S, -kk_t)
        S = (S * w_t[:, :, None, :] + sa[..., None] * (kk_t * a_t)[:, :, None, :]
             + v_t[..., None] * k_t[:, :, None, :])
        return S, jnp.einsum('bhvk,bhk->bhv', S, r_t)

    seq = tuple(jnp.swapaxes(t.astype(dt_), 0, 1) for t in (r, decay, k, v, kk, a))
    S, y = lax.scan(step, S0.astype(dt_), seq)
    return jnp.swapaxes(y, 0, 1), S


def rwkv_branch(p, prev_row, S0, mu, w0, w2, a0, a2, g2, k_k, k_a, r_k, ln_g, ln_b):
    n, T, _ = p.shape
    p_prev = jnp.concatenate([prev_row[:, None].astype(p.dtype), p[:, :-1]], axis=1)
    xs = p + (p_prev - p) * mu
    r, k, v, lw, la, lg = split_cols(xs, RWKV_SPLITS)
    w_log = -jax.nn.softplus(-(w0 + jnp.tanh(lw) @ w2)) - 0.5
    decay = jnp.exp(-jnp.exp(w_log))
    a = jax.nn.sigmoid(a0 + la @ a2)
    g = jax.nn.sigmoid(lg) @ g2
    heads = lambda t: t.reshape(n, T, RWKV_HEADS, HEAD_DIM)
    kk = heads(k * k_k)
    kk = kk * lax.rsqrt(jnp.maximum(jnp.sum(kk * kk, axis=-1, keepdims=True), 1e-12))
    k = k * (1.0 + (a - 1.0) * k_a)
    r, k, v, a, decay = heads(r), heads(k), heads(v), heads(a), heads(decay)
    y, S_new = rwkv_recurrence(r, decay, k, v, kk, a, S0)
    yf = y.astype(jnp.float32)
    m = jnp.mean(yf, axis=-1, keepdims=True)
    var = jnp.mean(jnp.square(yf - m), axis=-1, keepdims=True)
    yn = ((yf - m) * lax.rsqrt(var + RWKV_GN_EPS)).reshape(n, T, RWKV_WIDTH)
    yn = (yn * ln_g + ln_b).astype(y.dtype)
    bonus = (jnp.sum(r * k * r_k, axis=-1, keepdims=True) * v).reshape(n, T, RWKV_WIDTH)
    return (yn + bonus) * g, S_new, p[:, -1]


def trunk_layer(x, pos0, k_past, v_past, ssm_S, ssm_conv, rwkv_S, rwkv_shift, ffn_conv,
                norm1_g, w_in, ssm_conv_w, ssm_conv_b, ssm_dt_bias, ssm_A_log, ssm_D, ssm_norm_g,
                rwkv_mu, rwkv_w0, rwkv_w2, rwkv_a0, rwkv_a2, rwkv_g2, rwkv_k_k, rwkv_k_a, rwkv_r_k,
                rwkv_ln_g, rwkv_ln_b, w_br_att, w_br_ssm, w_br_rwkv, w_out, norm2_g, w_up,
                ffn_conv_w, ffn_conv_b, w_down):
    n, T, _ = x.shape
    h = rms_norm(x, norm1_g)
    gates, q, k, v, z, xbc, dt_raw, rw = split_cols(h @ w_in, IN_SPLITS)
    pos = pos0 + jnp.arange(T)
    q = rope(q.reshape(n, T, ATT_HEADS, HEAD_DIM), pos)
    k = rope(k.reshape(n, T, ATT_HEADS, HEAD_DIM), pos)
    v = v.reshape(n, T, ATT_HEADS, HEAD_DIM)
    o_att = moba_attention(q, k_past, k, v_past, v, pos0).reshape(n, T, ATT_WIDTH)
    o_ssm, ssm_S_new, ssm_conv_new = ssm_branch(z, xbc, dt_raw, ssm_conv, ssm_S, ssm_conv_w, ssm_conv_b,
                                                ssm_dt_bias, ssm_A_log, ssm_D, ssm_norm_g)
    o_rwkv, rwkv_S_new, rwkv_shift_new = rwkv_branch(rw, rwkv_shift, rwkv_S, rwkv_mu, rwkv_w0, rwkv_w2,
                                                     rwkv_a0, rwkv_a2, rwkv_g2, rwkv_k_k, rwkv_k_a,
                                                     rwkv_r_k, rwkv_ln_g, rwkv_ln_b)
    g = jax.nn.sigmoid(gates).reshape(n, T, N_BRANCH, D_MODEL)
    merged = (g[:, :, 0] * (o_att @ w_br_att) + g[:, :, 1] * (o_ssm @ w_br_ssm)
              + g[:, :, 2] * (o_rwkv @ w_br_rwkv))
    x = x + merged @ w_out
    h2 = rms_norm(x, norm2_g)
    g_pre, u = jnp.split(h2 @ w_up, 2, axis=-1)
    g_conv, ffn_conv_new = causal_dwconv(g_pre, ffn_conv, ffn_conv_w, ffn_conv_b)
    x = x + (jax.nn.silu(g_conv) * u) @ w_down
    return x, (k, v, ssm_S_new, ssm_conv_new, rwkv_S_new, rwkv_shift_new, ffn_conv_new)


def setup_inputs(seed: int = 0) -> dict:
    key = jax.random.key(seed)
    ks = iter(jax.random.split(key, 48))

    def nrm(shape, scale):
        return scale * jax.random.normal(next(ks), shape, jnp.float32)

    def unif(shape, lo, hi):
        return jax.random.uniform(next(ks), shape, jnp.float32, lo, hi)

    L = DEPTH
    n_pages = PAST_LEN // PAGE_SIZE
    n_used = DEC_BATCH * n_pages
    n_phys = n_used + -(-n_used // 4)
    page_table = jax.random.permutation(next(ks), n_phys)[:n_used].reshape(DEC_BATCH, n_pages).astype(jnp.int32)
    dt0 = jnp.exp(unif((L, SSM_HEADS), math.log(1e-3), math.log(1e-1)))
    return {
        'x_prompt': nrm((BATCH, SEQ, D_MODEL), 1.0),
        'x_sample': nrm((DEC_BATCH, DEC_SEQ, D_MODEL), 1.0),
        'cache_k': nrm((L, n_phys, PAGE_SIZE, ATT_HEADS, HEAD_DIM), 1.0),
        'cache_v': nrm((L, n_phys, PAGE_SIZE, ATT_HEADS, HEAD_DIM), 1.0),
        'page_table': page_table,
        'state_ssm': nrm((L, DEC_BATCH, SSM_HEADS, SSM_STATE, SSM_HEAD_DIM), 0.1),
        'state_ssm_conv': nrm((L, DEC_BATCH, SSM_CONV - 1, SSM_CONV_DIM), 1.0),
        'state_rwkv': nrm((L, DEC_BATCH, RWKV_HEADS, HEAD_DIM, HEAD_DIM), 0.1),
        'state_rwkv_shift': nrm((L, DEC_BATCH, RWKV_SHIFT_DIM), 1.0),
        'state_ffn_conv': nrm((L, DEC_BATCH, FFN_CONV - 1, D_FF), 1.0),
        'norm1_g': 1.0 + nrm((L, D_MODEL), 0.02),
        'w_in': nrm((L, D_MODEL, IN_WIDTH), D_MODEL ** -0.5),
        'ssm_conv_w': nrm((L, SSM_CONV, SSM_CONV_DIM), SSM_CONV ** -0.5),
        'ssm_conv_b': nrm((L, SSM_CONV_DIM), 0.02),
        'ssm_dt_bias': dt0 + jnp.log(-jnp.expm1(-dt0)),
        'ssm_A_log': jnp.log(unif((L, SSM_HEADS), 1.0, 16.0)),
        'ssm_D': 1.0 + nrm((L, SSM_HEADS), 0.1),
        'ssm_norm_g': 1.0 + nrm((L, SSM_INNER), 0.02),
        'rwkv_mu': unif((L, RWKV_SHIFT_DIM), 0.0, 1.0),
        'rwkv_w0': -1.0 + nrm((L, RWKV_WIDTH), 0.5),
        'rwkv_w2': nrm((L, DECAY_LORA, RWKV_WIDTH), 0.5 * DECAY_LORA ** -0.5),
        'rwkv_a0': nrm((L, RWKV_WIDTH), 0.1),
        'rwkv_a2': nrm((L, ICLR_LORA, RWKV_WIDTH), 0.5 * ICLR_LORA ** -0.5),
        'rwkv_g2': nrm((L, GATE_LORA, RWKV_WIDTH), GATE_LORA ** -0.5),
        'rwkv_k_k': 0.85 + nrm((L, RWKV_WIDTH), 0.05),
        'rwkv_k_a': 1.0 + nrm((L, RWKV_WIDTH), 0.05),
        'rwkv_r_k': nrm((L, RWKV_HEADS, HEAD_DIM), 0.1),
        'rwkv_ln_g': 1.0 + nrm((L, RWKV_WIDTH), 0.02),
        'rwkv_ln_b': nrm((L, RWKV_WIDTH), 0.02),
        'w_br_att': nrm((L, ATT_WIDTH, D_MODEL), ATT_WIDTH ** -0.5),
        'w_br_ssm': nrm((L, SSM_INNER, D_MODEL), SSM_INNER ** -0.5),
        'w_br_rwkv': nrm((L, RWKV_WIDTH, D_MODEL), RWKV_WIDTH ** -0.5),
        'w_out': nrm((L, D_MODEL, D_MODEL), D_MODEL ** -0.5),
        'norm2_g': 1.0 + nrm((L, D_MODEL), 0.02),
        'w_up': nrm((L, D_MODEL, 2 * D_FF), D_MODEL ** -0.5),
        'ffn_conv_w': nrm((L, FFN_CONV, D_FF), FFN_CONV ** -0.5),
        'ffn_conv_b': nrm((L, D_FF), 0.02),
        'w_down': nrm((L, D_FF, D_MODEL), D_FF ** -0.5),
        'final_norm_g': 1.0 + nrm((D_MODEL,), 0.02),
    }


def reference(x_prompt, x_sample, cache_k, cache_v, page_table, state_ssm, state_ssm_conv,
              state_rwkv, state_rwkv_shift, state_ffn_conv, norm1_g, w_in, ssm_conv_w, ssm_conv_b,
              ssm_dt_bias, ssm_A_log, ssm_D, ssm_norm_g, rwkv_mu, rwkv_w0, rwkv_w2, rwkv_a0, rwkv_a2,
              rwkv_g2, rwkv_k_k, rwkv_k_a, rwkv_r_k, rwkv_ln_g, rwkv_ln_b, w_br_att, w_br_ssm,
              w_br_rwkv, w_out, norm2_g, w_up, ffn_conv_w, ffn_conv_b, w_down, final_norm_g):
    b_p = x_prompt.shape[0]
    b_s = x_sample.shape[0]
    dt_ = x_prompt.dtype
    n_pages = page_table.shape[1]
    past_len = n_pages * cache_k.shape[2]
    xp, xs = x_prompt, x_sample
    new_p, new_s = [], []
    for l in range(DEPTH):
        params = (norm1_g[l], w_in[l], ssm_conv_w[l], ssm_conv_b[l], ssm_dt_bias[l], ssm_A_log[l],
                  ssm_D[l], ssm_norm_g[l], rwkv_mu[l], rwkv_w0[l], rwkv_w2[l], rwkv_a0[l], rwkv_a2[l],
                  rwkv_g2[l], rwkv_k_k[l], rwkv_k_a[l], rwkv_r_k[l], rwkv_ln_g[l], rwkv_ln_b[l],
                  w_br_att[l], w_br_ssm[l], w_br_rwkv[l], w_out[l], norm2_g[l], w_up[l],
                  ffn_conv_w[l], ffn_conv_b[l], w_down[l])
        xp, st_p = trunk_layer(
            xp, 0,
            jnp.zeros((b_p, 0, ATT_HEADS, HEAD_DIM), dt_), jnp.zeros((b_p, 0, ATT_HEADS, HEAD_DIM), dt_),
            jnp.zeros((b_p, SSM_HEADS, SSM_STATE, SSM_HEAD_DIM), dt_),
            jnp.zeros((b_p, SSM_CONV - 1, SSM_CONV_DIM), dt_),
            jnp.zeros((b_p, RWKV_HEADS, HEAD_DIM, HEAD_DIM), dt_),
            jnp.zeros((b_p, RWKV_SHIFT_DIM), dt_),
            jnp.zeros((b_p, FFN_CONV - 1, D_FF), dt_),
            *params)
        k_past = cache_k[l][page_table].reshape(b_s, past_len, ATT_HEADS, HEAD_DIM)
        v_past = cache_v[l][page_table].reshape(b_s, past_len, ATT_HEADS, HEAD_DIM)
        xs, st_s = trunk_layer(xs, past_len, k_past, v_past, state_ssm[l], state_ssm_conv[l],
                               state_rwkv[l], state_rwkv_shift[l], state_ffn_conv[l], *params)
        new_p.append(st_p)
        new_s.append(st_s)
    y_prompt = rms_norm(xp, final_norm_g)
    y_sample = rms_norm(xs, final_norm_g)

    def stacked(states, i):
        return jnp.stack([s[i] for s in states])

    k_prompt, v_prompt = stacked(new_p, 0), stacked(new_p, 1)
    k_sample, v_sample = stacked(new_s, 0), stacked(new_s, 1)
    ssm_prompt, ssm_sample = stacked(new_p, 2), stacked(new_s, 2)
    ssm_conv_prompt, ssm_conv_sample = stacked(new_p, 3), stacked(new_s, 3)
    rwkv_prompt, rwkv_sample = stacked(new_p, 4), stacked(new_s, 4)
    rwkv_shift_prompt, rwkv_shift_sample = stacked(new_p, 5), stacked(new_s, 5)
    ffn_conv_prompt, ffn_conv_sample = stacked(new_p, 6), stacked(new_s, 6)
    return (y_prompt, y_sample, k_prompt, v_prompt, k_sample, v_sample, ssm_prompt, ssm_sample,
            ssm_conv_prompt, ssm_conv_sample, rwkv_prompt, rwkv_sample, rwkv_shift_prompt,
            rwkv_shift_sample, ffn_conv_prompt, ffn_conv_sample)
```

```python
import functools
import math

import jax
import jax.numpy as jnp
import numpy as np
from jax import lax
from jax.experimental import pallas as pl
from jax.experimental.pallas import tpu as pltpu

F32 = jnp.float32
BF16 = jnp.bfloat16

D_MODEL = 1024
DEPTH = 4
PAGE_SIZE = 128
MIX_WIDTH = D_MODEL // 2
HEAD_DIM = 64
N_HEADS = MIX_WIDTH // HEAD_DIM
N_PAIRS = N_HEADS // 2
MOBA_BLOCK = 256
MOBA_TOPK = 3
ROPE_THETA = 10000.0
SSM_GROUPS = 2
SSM_STATE = 128
SSM_CONV = 4
SSM_CHUNK = 128
SSM_CONV_DIM = MIX_WIDTH + 2 * SSM_GROUPS * SSM_STATE
DECAY_LORA = 64
ICLR_LORA = 64
GATE_LORA = 128
RWKV_SHIFT_DIM = 3 * MIX_WIDTH + DECAY_LORA + ICLR_LORA + GATE_LORA
RWKV_GN_EPS = 64e-5
RWKV_CHUNK = 64
D_FF = ((8 * D_MODEL // 3 + 127) // 128) * 128
FFN_CONV = 3
EPS = 1e-6
LANES = 128
SUBLANES = 8
NEG = -1e30
VMEM_LIMIT = 52 * 1024 * 1024

COL_RW = 0
COL_DT = RWKV_SHIFT_DIM
COL_GATES = 2048
COL_XBC = COL_GATES + 3 * D_MODEL
COL_Z = COL_XBC + SSM_CONV_DIM
COL_V = COL_Z + MIX_WIDTH
MAIN_WIDTH = COL_V + MIX_WIDTH


def _cparams(*sem):
    return pltpu.CompilerParams(dimension_semantics=sem, vmem_limit_bytes=VMEM_LIMIT)


def _rms(x, g):
    return x * lax.rsqrt(jnp.mean(x * x, axis=-1, keepdims=True) + EPS) * g


def _dot(a, b):
    return jnp.dot(a.astype(BF16), b.astype(BF16), preferred_element_type=F32)


def _dot_nt(a, b):
    return lax.dot_general(a.astype(BF16), b.astype(BF16), (((1,), (1,)), ((), ())), preferred_element_type=F32)


def _dot_tn(a, b):
    return lax.dot_general(a.astype(BF16), b.astype(BF16), (((0,), (0,)), ((), ())), preferred_element_type=F32)


def _dot_f32(a, b):
    return jnp.dot(a, b, preferred_element_type=F32, precision=lax.Precision.HIGHEST)


def _silu(x):
    return x * jax.nn.sigmoid(x)


def _softplus(x):
    return jnp.maximum(x, 0.0) + jnp.log(1.0 + jnp.exp(-jnp.abs(x)))


def _shift_rows(g, k, prevs, period):
    rows = g.shape[0]
    out = pltpu.roll(g, k, axis=0)
    tig = lax.broadcasted_iota(jnp.int32, (rows, 1), 0) & (period - 1)
    for t0 in range(k):
        out = jnp.where(tig == t0, prevs[k - t0 - 1], out)
    return out


def _norm_matmul_kernel(x_ref, g_ref, w_ref, o_ref):
    o_ref[...] = _dot(_rms(x_ref[...], g_ref[...]), w_ref[...])


def _norm_matmul(x, g, w, tm, tn):
    m, d = x.shape
    n = w.shape[1]
    return pl.pallas_call(
        _norm_matmul_kernel,
        grid=(n // tn, m // tm),
        in_specs=[pl.BlockSpec((tm, d), lambda j, i: (i, 0)),
                  pl.BlockSpec((1, d), lambda j, i: (0, 0)),
                  pl.BlockSpec((d, tn), lambda j, i: (0, j))],
        out_specs=pl.BlockSpec((tm, tn), lambda j, i: (i, j)),
        out_shape=jax.ShapeDtypeStruct((m, n), F32),
        compiler_params=_cparams("arbitrary", "arbitrary"),
        name="norm_matmul",
    )(x, g, w)


def _rope_proj_kernel(x_ref, g_ref, w_ref, wr_ref, cos_ref, sin_ref, o_ref):
    h = _rms(x_ref[...], g_ref[...]).astype(BF16)
    a = jnp.dot(h, w_ref[...], preferred_element_type=F32)
    b = jnp.dot(h, wr_ref[...], preferred_element_type=F32)
    reps = a.shape[1] // LANES
    cos = jnp.concatenate([cos_ref[...]] * reps, axis=1)
    sin = jnp.concatenate([sin_ref[...]] * reps, axis=1)
    o_ref[...] = a * cos + b * sin


def _rope_proj(x, g, w, wr, cos, sin, tm):
    m, d = x.shape
    n = w.shape[1]
    tab_blocks = cos.shape[0] // tm
    return pl.pallas_call(
        _rope_proj_kernel,
        grid=(m // tm,),
        in_specs=[pl.BlockSpec((tm, d), lambda i: (i, 0)),
                  pl.BlockSpec((1, d), lambda i: (0, 0)),
                  pl.BlockSpec((d, n), lambda i: (0, 0)),
                  pl.BlockSpec((d, n), lambda i: (0, 0)),
                  pl.BlockSpec((tm, LANES), lambda i: (i % tab_blocks, 0)),
                  pl.BlockSpec((tm, LANES), lambda i: (i % tab_blocks, 0))],
        out_specs=pl.BlockSpec((tm, n), lambda i: (i, 0)),
        out_shape=jax.ShapeDtypeStruct((m, n), F32),
        compiler_params=_cparams("arbitrary"),
        name="rope_proj",
    )(x, g, w, wr, cos, sin)


def _kmean_kernel(k_ref, o_ref):
    k = k_ref[...]
    o_ref[...] = jnp.mean(k.reshape(SUBLANES, MOBA_BLOCK, k.shape[-1]), axis=1)


def _kmean(k2d, n_seq, seq):
    nb = seq // MOBA_BLOCK
    w = k2d.shape[1] // 2
    rows = SUBLANES * MOBA_BLOCK
    return pl.pallas_call(
        _kmean_kernel,
        grid=(n_seq * nb // SUBLANES,),
        in_specs=[pl.BlockSpec((rows, w), lambda i: (i, 1))],
        out_specs=pl.BlockSpec((SUBLANES, w), lambda i: (i, 0)),
        out_shape=jax.ShapeDtypeStruct((n_seq * nb, w), F32),
        compiler_params=_cparams("arbitrary"),
        name="moba_kmean",
    )(k2d).reshape(n_seq, nb, w)


def _head_mask(pair_vals, hl):
    lane = lax.broadcasted_iota(jnp.int32, pair_vals.shape, 1)
    return jnp.where((lane >= hl * HEAD_DIM) & (lane < (hl + 1) * HEAD_DIM), pair_vals, jnp.zeros_like(pair_vals))


def _moba_select_kernel(q_ref, km_ref, o_ref, *, nb):
    tq = q_ref.shape[0]
    q0 = pl.program_id(1) * tq
    qpos = q0 + lax.broadcasted_iota(jnp.int32, (1, tq), 1)
    own = qpos // MOBA_BLOCK
    kb = lax.broadcasted_iota(jnp.int32, (nb, tq), 0)
    kbf = kb.astype(F32)
    valid = kb < own
    for h in range(N_HEADS):
        p, hl = divmod(h, 2)
        lanes = slice(p * LANES, (p + 1) * LANES)
        km = _head_mask(km_ref[:, lanes], hl)
        gate = lax.dot_general(km, q_ref[:, lanes], (((1,), (1,)), ((), ())),
                               preferred_element_type=F32, precision=lax.Precision.HIGHEST)
        g = jnp.where(valid, gate, -jnp.inf)
        sel = jnp.zeros((nb, tq), jnp.bool_)
        for _ in range(MOBA_TOPK):
            m = jnp.max(g, axis=0, keepdims=True)
            first = jnp.min(jnp.where(g == m, kbf, float(nb)), axis=0, keepdims=True)
            hit = (kbf == first) & (m > -jnp.inf)
            sel = sel | hit
            g = jnp.where(hit, -jnp.inf, g)
        o_ref[h] = jnp.where(sel, 0.0, NEG)


def _moba_select(qk2d, kmean, n_seq, seq, tq):
    nb = seq // MOBA_BLOCK
    w = qk2d.shape[1] // 2
    tiles = seq // tq
    return pl.pallas_call(
        functools.partial(_moba_select_kernel, nb=nb),
        grid=(n_seq, tiles),
        in_specs=[pl.BlockSpec((tq, w), lambda b, i: (b * tiles + i, 0)),
                  pl.BlockSpec((None, nb, w), lambda b, i: (b, 0, 0))],
        out_specs=pl.BlockSpec((None, N_HEADS, nb, tq), lambda b, i: (b, 0, 0, i)),
        out_shape=jax.ShapeDtypeStruct((n_seq, N_HEADS, nb, seq), F32),
        compiler_params=_cparams("arbitrary", "arbitrary"),
        name="moba_select",
    )(qk2d, kmean)


def _moba_prompt_kernel(q_ref, k_ref, vt_ref, bias_ref, o_ref):
    tq = q_ref.shape[0]
    j = pl.program_id(1)
    scale = HEAD_DIM ** -0.5
    key_i = lax.broadcasted_iota(jnp.int32, (MOBA_BLOCK, tq), 0)
    qry_i = lax.broadcasted_iota(jnp.int32, (MOBA_BLOCK, tq), 1)
    causal = key_i <= qry_i
    for h in range(N_HEADS):
        p, hl = divmod(h, 2)
        lanes = slice(p * LANES, (p + 1) * LANES)
        rows = slice(h * HEAD_DIM, (h + 1) * HEAD_DIM)
        qm = _head_mask((q_ref[:, lanes] * scale).astype(BF16), hl)
        s = jnp.where(causal, _dot_nt(k_ref[j, :, lanes], qm), NEG)
        m = jnp.max(s, axis=0, keepdims=True)
        pt = jnp.exp(s - m)
        l = jnp.sum(pt, axis=0, keepdims=True)
        acc = _dot(vt_ref[j, rows, :], pt)

        def body(kb, carry, h=h, lanes=lanes, rows=rows, qm=qm):
            m, l, acc = carry
            s = _dot_nt(k_ref[kb, :, lanes], qm) + bias_ref[h, pl.ds(kb, 1), :]
            m_new = jnp.maximum(m, jnp.max(s, axis=0, keepdims=True))
            alpha = jnp.exp(m - m_new)
            pt = jnp.exp(s - m_new)
            l = l * alpha + jnp.sum(pt, axis=0, keepdims=True)
            acc = acc * alpha + _dot(vt_ref[kb, rows, :], pt)
            return m_new, l, acc

        m, l, acc = lax.fori_loop(0, j, body, (m, l, acc))
        o_ref[rows, :] = acc / l


def _moba_prompt(qk2d, k_blocks, vt_blocks, bias, n_seq, seq):
    nb = seq // MOBA_BLOCK
    w = MIX_WIDTH
    tq = MOBA_BLOCK
    return pl.pallas_call(
        _moba_prompt_kernel,
        grid=(n_seq, nb),
        in_specs=[pl.BlockSpec((tq, w), lambda b, i: (b * nb + i, 0)),
                  pl.BlockSpec((None, nb, MOBA_BLOCK, w), lambda b, i: (b, 0, 0, 0)),
                  pl.BlockSpec((None, nb, w, MOBA_BLOCK), lambda b, i: (b, 0, 0, 0)),
                  pl.BlockSpec((None, N_HEADS, nb, tq), lambda b, i: (b, 0, 0, i))],
        out_specs=pl.BlockSpec((None, w, tq), lambda b, i: (b, 0, i)),
        out_shape=jax.ShapeDtypeStruct((n_seq, w, seq), F32),
        compiler_params=_cparams("arbitrary", "arbitrary"),
        name="moba_prompt",
    )(qk2d, k_blocks, vt_blocks, bias)


def _moba_sample_kernel(pt_ref, qbd_ref, ka_ref, kb_ref, va_ref, vb_ref, kn_ref, vn_ref, o_ref,
                        r_scr, m_scr, l_scr, g_scr, *, nb):
    del pt_ref
    b = pl.program_id(1)
    nq = qbd_ref.shape[0]
    lane = lax.broadcasted_iota(jnp.int32, (nq, LANES), 1)

    @pl.when(b == 0)
    def _():
        m_scr[...] = jnp.full(m_scr.shape, NEG, F32)
        l_scr[...] = jnp.zeros(l_scr.shape, F32)
        g_scr[...] = jnp.full(g_scr.shape, -jnp.inf, F32)

    qbd = qbd_ref[...]
    k = jnp.concatenate([ka_ref[...], kb_ref[...]], axis=0)
    v = jnp.concatenate([va_ref[...], vb_ref[...]], axis=0)
    kmean = jnp.mean(k, axis=0, keepdims=True)
    gate = jnp.sum(qbd * kmean, axis=-1, keepdims=True)
    s = _dot_nt(qbd, k)
    m_b = jnp.max(s, axis=-1, keepdims=True)
    p = jnp.exp(s - m_b)
    l_b = jnp.sum(p, axis=-1, keepdims=True)
    r_scr[b] = _dot(p, v)
    col = lane == b
    m_scr[...] = jnp.where(col, m_b, m_scr[...])
    l_scr[...] = jnp.where(col, l_b, l_scr[...])
    g_scr[...] = jnp.where(col, gate, g_scr[...])

    @pl.when(b == nb - 1)
    def _():
        lane_f = lane.astype(F32)
        g = g_scr[...]
        sel = jnp.zeros((nq, LANES), jnp.bool_)
        for _ in range(MOBA_TOPK):
            mx = jnp.max(g, axis=-1, keepdims=True)
            first = jnp.min(jnp.where(g == mx, lane_f, float(LANES)), axis=-1, keepdims=True)
            hit = (lane_f == first) & (mx > -jnp.inf)
            sel = sel | hit
            g = jnp.where(hit, -jnp.inf, g)
        n_new = kn_ref.shape[0]
        s_own = _dot_nt(qbd, kn_ref[...])
        q_t = lax.broadcasted_iota(jnp.int32, (nq, n_new), 0) & (n_new - 1)
        k_t = lax.broadcasted_iota(jnp.int32, (nq, n_new), 1)
        s_own = jnp.where(k_t <= q_t, s_own, NEG)
        m_all = m_scr[...]
        m_tot = jnp.maximum(jnp.max(jnp.where(sel, m_all, NEG), axis=-1, keepdims=True),
                            jnp.max(s_own, axis=-1, keepdims=True))
        w_sel = jnp.where(sel, jnp.exp(m_all - m_tot), 0.0)
        p_own = jnp.exp(s_own - m_tot)
        l_tot = jnp.sum(w_sel * l_scr[...], axis=-1, keepdims=True) + jnp.sum(p_own, axis=-1, keepdims=True)
        acc = _dot(p_own, vn_ref[...])
        for bb in range(nb):
            acc = acc + w_sel[:, bb:bb + 1] * r_scr[bb]
        acc = acc / l_tot
        lane_w = lax.broadcasted_iota(jnp.int32, (n_new, acc.shape[1]), 1) // HEAD_DIM
        out = jnp.zeros((n_new, acc.shape[1]), F32)
        for h in range(N_HEADS):
            out = out + jnp.where(lane_w == h, acc[h * n_new:(h + 1) * n_new, :], 0.0)
        o_ref[...] = out


def _moba_sample(page_table, qbd, cache_k, cache_v, k_new, v_new):
    n_seq, nq, w = qbd.shape
    t_new = k_new.shape[1]
    n_pages = page_table.shape[1]
    nb = n_pages * PAGE_SIZE // MOBA_BLOCK
    pages_per_block = MOBA_BLOCK // PAGE_SIZE
    assert pages_per_block == 2 and nb <= LANES
    page = lambda off: pl.BlockSpec((None, PAGE_SIZE, w), lambda s, b, pt: (pt[s * n_pages + 2 * b + off], 0, 0))
    grid_spec = pltpu.PrefetchScalarGridSpec(
        num_scalar_prefetch=1,
        grid=(n_seq, nb),
        in_specs=[pl.BlockSpec((None, nq, w), lambda s, b, pt: (s, 0, 0)),
                  page(0), page(1), page(0), page(1),
                  pl.BlockSpec((None, t_new, w), lambda s, b, pt: (s, 0, 0)),
                  pl.BlockSpec((None, t_new, w), lambda s, b, pt: (s, 0, 0))],
        out_specs=pl.BlockSpec((None, t_new, w), lambda s, b, pt: (s, 0, 0)),
        scratch_shapes=[pltpu.VMEM((nb, nq, w), F32), pltpu.VMEM((nq, LANES), F32),
                        pltpu.VMEM((nq, LANES), F32), pltpu.VMEM((nq, LANES), F32)],
    )
    return pl.pallas_call(
        functools.partial(_moba_sample_kernel, nb=nb),
        grid_spec=grid_spec,
        out_shape=jax.ShapeDtypeStruct((n_seq, t_new, w), F32),
        compiler_params=_cparams("arbitrary", "arbitrary"),
        name="moba_sample",
    )(page_table.reshape(-1), qbd, cache_k, cache_k, cache_v, cache_v, k_new, v_new)


def _ssd_kernel(xbc_ref, z_ref, dtc_ref, dtt_ref, conv0_ref, s0_ref, cw_ref, cb_ref, dtb_c_ref, dtb_r_ref,
                a_c_ref, a_r_ref, dskip_ref, ng_ref, y_ref, sfin_ref, carry_scr, s_scr):
    c = pl.program_id(1)
    q = xbc_ref.shape[0]

    @pl.when(c == 0)
    def _():
        carry_scr[...] = conv0_ref[...]
        s_scr[...] = s0_ref[...]

    xbc = xbc_ref[...]
    carry = carry_scr[...]
    prevs = [carry[SUBLANES - i:SUBLANES - i + 1, :] for i in range(1, SSM_CONV)]
    conv = cb_ref[...] + cw_ref[SSM_CONV - 1:SSM_CONV, :] * xbc
    for k in range(1, SSM_CONV):
        conv = conv + cw_ref[SSM_CONV - 1 - k:SSM_CONV - k, :] * _shift_rows(xbc, k, prevs, q)
    carry_scr[...] = xbc[q - SUBLANES:, :]
    conv = _silu(conv)
    xs = conv[:, :MIX_WIDTH]
    gw = SSM_STATE
    bmat = [conv[:, MIX_WIDTH + g * gw:MIX_WIDTH + (g + 1) * gw] for g in range(SSM_GROUPS)]
    cmat = [conv[:, MIX_WIDTH + (SSM_GROUPS + g) * gw:MIX_WIDTH + (SSM_GROUPS + g + 1) * gw] for g in range(SSM_GROUPS)]

    dt_c = _softplus(dtc_ref[:, :N_HEADS] + dtb_r_ref[...])
    dt_t = _softplus(dtt_ref[...] + dtb_c_ref[...])
    a_r = -jnp.exp(a_r_ref[...])
    a_c = -jnp.exp(a_c_ref[...])
    ri = lax.broadcasted_iota(jnp.int32, (q, q), 0)
    ci = lax.broadcasted_iota(jnp.int32, (q, q), 1)
    lower = (ri >= ci).astype(F32)
    cum_c = _dot_f32(lower, dt_c * a_r)
    cum_t = _dot_f32(dt_t * a_c, (ri <= ci).astype(F32))
    causal = ri >= ci
    cb = [_dot_nt(cmat[g], bmat[g]) for g in range(SSM_GROUPS)]
    heads_per_group = N_HEADS // SSM_GROUPS
    lane = lax.broadcasted_iota(jnp.int32, (1, LANES), 1)
    first_half = lane < HEAD_DIM

    for p in range(N_PAIRS):
        lanes = slice(p * LANES, (p + 1) * LANES)
        xs_p = xs[:, lanes]
        s_in = s_scr[p]
        y_h, sc_h, cd_h = [], [], []
        for hl in range(2):
            h = 2 * p + hl
            g = h // heads_per_group
            cc = cum_c[:, h:h + 1]
            ct = cum_t[h:h + 1, :]
            lmat = jnp.where(causal, jnp.exp(jnp.where(causal, cc - ct, 0.0)), 0.0)
            y_diag = _dot(cb[g] * lmat * dt_t[h:h + 1, :], xs_p)
            y_off = _dot(cmat[g] * jnp.exp(cc), s_in)
            y_h.append(y_diag + y_off)
            dec_end = jnp.exp(cc[q - 1:q, :] - cc)
            sc_h.append(_dot_tn(bmat[g] * (dec_end * dt_c[:, h:h + 1]), xs_p))
            cd_h.append(jnp.exp(cc[q - 1:q, :]))
        y_ref[:, lanes] = jnp.where(first_half, y_h[0], y_h[1])
        s_scr[p] = (s_in * jnp.where(first_half, cd_h[0], cd_h[1])
                    + jnp.where(first_half, sc_h[0], sc_h[1]))

    y = (y_ref[...] + dskip_ref[...] * xs) * _silu(z_ref[...])
    gwid = MIX_WIDTH // SSM_GROUPS
    for g in range(SSM_GROUPS):
        cols = slice(g * gwid, (g + 1) * gwid)
        y_ref[:, cols] = _rms(y[:, cols], ng_ref[:, cols])
    sfin_ref[...] = s_scr[...]


def _ssd(main, dt_t, conv0, s0, prm, n_seq, seq, chunk):
    nc = seq // chunk
    xw = SSM_CONV_DIM
    row = lambda b, c: b * nc + c
    const2 = lambda b, c: (0, 0)
    return pl.pallas_call(
        _ssd_kernel,
        grid=(n_seq, nc),
        in_specs=[pl.BlockSpec((chunk, xw), lambda b, c: (row(b, c), COL_XBC // xw)),
                  pl.BlockSpec((chunk, MIX_WIDTH), lambda b, c: (row(b, c), COL_Z // MIX_WIDTH)),
                  pl.BlockSpec((chunk, LANES), lambda b, c: (row(b, c), COL_DT // LANES)),
                  pl.BlockSpec((None, N_HEADS, chunk), lambda b, c: (b, 0, c)),
                  pl.BlockSpec((None, SUBLANES, xw), lambda b, c: (b, 0, 0)),
                  pl.BlockSpec((None, N_PAIRS, SSM_STATE, LANES), lambda b, c: (b, 0, 0, 0)),
                  pl.BlockSpec((SSM_CONV, xw), const2),
                  pl.BlockSpec((1, xw), const2),
                  pl.BlockSpec((N_HEADS, 1), const2),
                  pl.BlockSpec((1, N_HEADS), const2),
                  pl.BlockSpec((N_HEADS, 1), const2),
                  pl.BlockSpec((1, N_HEADS), const2),
                  pl.BlockSpec((1, MIX_WIDTH), const2),
                  pl.BlockSpec((1, MIX_WIDTH), const2)],
        out_specs=[pl.BlockSpec((chunk, MIX_WIDTH), lambda b, c: (row(b, c), 0)),
                   pl.BlockSpec((None, N_PAIRS, SSM_STATE, LANES), lambda b, c: (b, 0, 0, 0))],
        out_shape=[jax.ShapeDtypeStruct((n_seq * seq, MIX_WIDTH), F32),
                   jax.ShapeDtypeStruct((n_seq, N_PAIRS, SSM_STATE, LANES), F32)],
        scratch_shapes=[pltpu.VMEM((SUBLANES, xw), F32), pltpu.VMEM((N_PAIRS, SSM_STATE, LANES), F32)],
        compiler_params=_cparams("arbitrary", "arbitrary"),
        name="ssd",
    )(main, main, main, dt_t, conv0, s0, prm["ssm_conv_w"], prm["ssm_conv_b"], prm["dtb_c"], prm["dtb_r"],
      prm["alog_c"], prm["alog_r"], prm["dskip"], prm["ssm_norm_g"])


def _rwkv_kernel(rw_ref, shift0_ref, s0_ref, mu_ref, w0_ref, w2_ref, a0_ref, a2_ref, g2_ref, kk_ref, ka_ref,
                 rk_ref, lng_ref, lnb_ref, seg_ref, y_ref, sfin_ref, carry_scr, s_scr):
    c = pl.program_id(1)
    n_t = rw_ref.shape[0]
    w = MIX_WIDTH

    @pl.when(c == 0)
    def _():
        carry_scr[...] = shift0_ref[...]
        s_scr[...] = s0_ref[...]

    p_in = rw_ref[...]
    p_prev = _shift_rows(p_in, 1, [carry_scr[SUBLANES - 1:SUBLANES, :]], n_t)
    carry_scr[...] = p_in[n_t - SUBLANES:, :]
    xs = p_in + (p_prev - p_in) * mu_ref[...]
    r = xs[:, 0:w]
    k = xs[:, w:2 * w]
    v = xs[:, 2 * w:3 * w]
    lora_in = xs[:, 3 * w:3 * w + LANES]
    gate_in = xs[:, 3 * w + LANES:]
    w_log = -_softplus(-(w0_ref[...] + _dot(jnp.tanh(lora_in), w2_ref[...]))) - 0.5
    logw = -jnp.exp(w_log)
    a = jax.nn.sigmoid(a0_ref[...] + _dot(lora_in, a2_ref[...]))
    g = _dot(jax.nn.sigmoid(gate_in), g2_ref[...])
    seg = seg_ref[...]
    kk = k * kk_ref[...]
    kk = kk * lax.rsqrt(jnp.maximum(_dot_f32(kk * kk, seg), 1e-12))
    k = k * (1.0 + (a - 1.0) * ka_ref[...])
    bonus = _dot_f32(r * k * rk_ref[...], seg) * v

    ri = lax.broadcasted_iota(jnp.int32, (n_t, n_t), 0)
    ci = lax.broadcasted_iota(jnp.int32, (n_t, n_t), 1)
    strict = ri > ci
    incl = ri >= ci
    eye = (ri == ci).astype(F32)
    cl = _dot_f32(incl.astype(F32), logw)
    p_inc = jnp.exp(cl)
    p_inv = jnp.exp(-cl)
    a_t = -kk * jnp.exp(cl - logw)
    b_t = kk * a * p_inv
    k_t = k * p_inv
    r_t = r * p_inc
    p_end = p_inc[n_t - 1:n_t, :]
    n_dbl = max(int(math.log2(n_t)) - 1, 0)
    first_half = lax.broadcasted_iota(jnp.int32, (1, LANES), 1) < HEAD_DIM
    eye_l = (lax.broadcasted_iota(jnp.int32, (LANES, LANES), 0)
             == lax.broadcasted_iota(jnp.int32, (LANES, LANES), 1)).astype(F32)
    per_head = []

    for h in range(N_HEADS):
        p, hl = divmod(h, 2)
        lanes = slice(p * LANES, (p + 1) * LANES)
        am =_head_mask(a_t[:, lanes], hl)
        rm = _head_mask(r_t[:, lanes], hl)
        bm = _head_mask(b_t[:, lanes], hl)
        km = _head_mask(k_t[:, lanes], hl)
        vp = v[:, lanes]
        gram = _dot_nt(jnp.concatenate([am, rm], axis=0), jnp.concatenate([bm, km], axis=0))
        l_ab = jnp.where(strict, gram[:n_t, :n_t], 0.0)
        l_ak = jnp.where(strict, gram[:n_t, n_t:], 0.0)
        m_rb = jnp.where(incl, gram[n_t:, :n_t], 0.0)
        m_rk = jnp.where(incl, gram[n_t:, n_t:], 0.0)
        t_inv = eye + l_ab
        l_pow = l_ab
        for _ in range(n_dbl):
            l_pow = _dot(l_pow, l_pow)
            t_inv = t_inv + _dot(t_inv, l_pow)
        wu = _dot(t_inv, jnp.concatenate([am, _dot(l_ak, vp)], axis=1))
        mw = _dot(m_rb, wu)
        q_hat = rm + mw[:, :LANES]
        y_hat = mw[:, LANES:] + _dot(m_rk, vp)
        bw = _dot_tn(wu, bm)
        per_head.append((q_hat, y_hat, bw[:LANES, :], bw[LANES:, :]))
        if hl == 0:
            continue
        (q0, y0, wb0, ub0), (q1, y1, wb1, ub1) = per_head
        per_head = []
        s_in = s_scr[p]
        y_ref[:, lanes] = _dot_nt(q0 + q1, s_in) + jnp.where(first_half, y0, y1)
        vk = _dot_tn(vp, k_t[:, lanes])
        top = lax.broadcasted_iota(jnp.int32, (LANES, LANES), 0) < HEAD_DIM
        diag = top == (lax.broadcasted_iota(jnp.int32, (LANES, LANES), 1) < HEAD_DIM)
        h_t = jnp.where(diag, jnp.where(top, ub0, ub1) + vk, 0.0)
        g_t = eye_l + wb0 + wb1
        s_scr[p] = (_dot(s_in, g_t) + h_t) * p_end[:, lanes]

    y = y_ref[...]
    mean = _dot_f32(y, seg) * (1.0 / HEAD_DIM)
    yc = y - mean
    var = _dot_f32(yc * yc, seg) * (1.0 / HEAD_DIM)
    yn = yc * lax.rsqrt(var + RWKV_GN_EPS) * lng_ref[...] + lnb_ref[...]
    y_ref[...] = (yn + bonus) * g
    sfin_ref[...] = s_scr[...]


def _rwkv(main, shift0, s0, prm, n_seq, seq, chunk):
    nc = seq // chunk
    rw = RWKV_SHIFT_DIM
    w = MIX_WIDTH
    const2 = lambda b, c: (0, 0)
    vec = pl.BlockSpec((1, w), const2)
    return pl.pallas_call(
        _rwkv_kernel,
        grid=(n_seq, nc),
        in_specs=[pl.BlockSpec((chunk, rw), lambda b, c: (b * nc + c, COL_RW // rw)),
                  pl.BlockSpec((None, SUBLANES, rw), lambda b, c: (b, 0, 0)),
                  pl.BlockSpec((None, N_PAIRS, LANES, LANES), lambda b, c: (b, 0, 0, 0)),
                  pl.BlockSpec((1, rw), const2),
                  vec, pl.BlockSpec((LANES, w), const2),
                  vec, pl.BlockSpec((LANES, w), const2),
                  pl.BlockSpec((GATE_LORA, w), const2),
                  vec, vec, vec, vec, vec,
                  pl.BlockSpec((w, w), const2)],
        out_specs=[pl.BlockSpec((chunk, w), lambda b, c: (b * nc + c, 0)),
                   pl.BlockSpec((None, N_PAIRS, LANES, LANES), lambda b, c: (b, 0, 0, 0))],
        out_shape=[jax.ShapeDtypeStruct((n_seq * seq, w), F32),
                   jax.ShapeDtypeStruct((n_seq, N_PAIRS, LANES, LANES), F32)],
        scratch_shapes=[pltpu.VMEM((SUBLANES, rw), F32), pltpu.VMEM((N_PAIRS, LANES, LANES), F32)],
        compiler_params=_cparams("arbitrary", "arbitrary"),
        name="rwkv7",
    )(main, shift0, s0, prm["rwkv_mu"], prm["rwkv_w0"], prm["w2_pad"], prm["rwkv_a0"], prm["a2_pad"],
      prm["rwkv_g2"], prm["rwkv_k_k"], prm["rwkv_k_a"], prm["rwkv_r_k"], prm["rwkv_ln_g"], prm["rwkv_ln_b"],
      prm["head_seg"])


def _merge_kernel(oa_ref, os_ref, or_ref, ga_ref, gs_ref, gr_ref, x_ref, wbr_ref, wo_ref, o_ref):
    merged = (jax.nn.sigmoid(ga_ref[...]) * _dot(oa_ref[...], wbr_ref[0])
              + jax.nn.sigmoid(gs_ref[...]) * _dot(os_ref[...], wbr_ref[1])
              + jax.nn.sigmoid(gr_ref[...]) * _dot(or_ref[...], wbr_ref[2]))
    o_ref[...] = x_ref[...] + _dot(merged, wo_ref[...])


def _merge(o_att, o_ssm, o_rwkv, main, x, w_br, w_out, tm):
    m, d = x.shape
    w = MIX_WIDTH
    act = pl.BlockSpec((tm, w), lambda i: (i, 0))
    gate = lambda b: pl.BlockSpec((tm, d), lambda i: (i, COL_GATES // d + b))
    return pl.pallas_call(
        _merge_kernel,
        grid=(m // tm,),
        in_specs=[act, act, act, gate(0), gate(1), gate(2),
                  pl.BlockSpec((tm, d), lambda i: (i, 0)),
                  pl.BlockSpec((3, w, d), lambda i: (0, 0, 0)),
                  pl.BlockSpec((d, d), lambda i: (0, 0))],
        out_specs=pl.BlockSpec((tm, d), lambda i: (i, 0)),
        out_shape=jax.ShapeDtypeStruct((m, d), F32),
        compiler_params=_cparams("arbitrary"),
        name="merge_out",
    )(o_att, o_ssm, o_rwkv, main, main, main, x, w_br, w_out)


def _ffn_kernel(x_ref, g_ref, wup_ref, cw_ref, cb_ref, wdn_ref, *rest, period, carried):
    if carried:
        o_ref, gp_ref, carry_scr = rest
    else:
        p1_ref, p2_ref, o_ref, gp_ref = rest
    x = x_ref[...]
    up = _dot(_rms(x, g_ref[...]), wup_ref[...])
    g_pre = up[:, :D_FF]
    u = up[:, D_FF:]
    gp_ref[...] = g_pre
    tm = x.shape[0]
    if carried:
        @pl.when(pl.program_id(1) == 0)
        def _():
            carry_scr[...] = jnp.zeros(carry_scr.shape, F32)
        prevs = [carry_scr[SUBLANES - i:SUBLANES - i + 1, :] for i in range(1, FFN_CONV)]
    else:
        prevs = [p1_ref[...], p2_ref[...]]
    conv = cb_ref[...] + cw_ref[FFN_CONV - 1:FFN_CONV, :] * g_pre
    for k in range(1, FFN_CONV):
        conv = conv + cw_ref[FFN_CONV - 1 - k:FFN_CONV - k, :] * _shift_rows(g_pre, k, prevs, period)
    if carried:
        carry_scr[...] = g_pre[tm - SUBLANES:, :]
    o_ref[...] = x + _dot(_silu(conv) * u, wdn_ref[...])


def _ffn(x, g, w_up, conv_w, conv_b, w_down, n_seq, seq, tm, prev=None):
    m, d = x.shape
    carried = prev is None
    const = lambda *_: (0, 0)
    if carried:
        tiles = seq // tm
        grid = (n_seq, tiles)
        row = lambda b, i: (b * tiles + i, 0)
        period = tm
        extra_in, extra_args = [], []
        scratch = [pltpu.VMEM((SUBLANES, D_FF), F32)]
    else:
        assert m == tm
        grid = (1, 1)
        row = lambda b, i: (0, 0)
        period = seq
        extra_in = [pl.BlockSpec((tm, D_FF), row)] * 2
        extra_args = list(prev)
        scratch = []
    return pl.pallas_call(
        functools.partial(_ffn_kernel, period=period, carried=carried),
        grid=grid,
        in_specs=[pl.BlockSpec((tm, d), row), pl.BlockSpec((1, d), const),
                  pl.BlockSpec((d, 2 * D_FF), const), pl.BlockSpec((FFN_CONV, D_FF), const),
                  pl.BlockSpec((1, D_FF), const), pl.BlockSpec((D_FF, d), const)] + extra_in,
        out_specs=[pl.BlockSpec((tm, d), row), pl.BlockSpec((tm, D_FF), row)],
        out_shape=[jax.ShapeDtypeStruct((m, d), F32), jax.ShapeDtypeStruct((m, D_FF), F32)],
        scratch_shapes=scratch,
        compiler_params=_cparams("arbitrary", "arbitrary"),
        name="conv_ffn",
    )(x, g, w_up, conv_w, conv_b, w_down, *extra_args)


def _final_norm_kernel(x_ref, g_ref, o_ref):
    o_ref[...] = _rms(x_ref[...], g_ref[...])


def _final_norm(x, g, tm):
    m, d = x.shape
    return pl.pallas_call(
        _final_norm_kernel,
        grid=(m // tm,),
        in_specs=[pl.BlockSpec((tm, d), lambda i: (i, 0)), pl.BlockSpec((1, d), lambda i: (0, 0))],
        out_specs=pl.BlockSpec((tm, d), lambda i: (i, 0)),
        out_shape=jax.ShapeDtypeStruct((m, d), F32),
        compiler_params=_cparams("arbitrary"),
        name="final_norm",
    )(x, g)


def _rot_half_cols(w):
    d, n = w.shape
    w4 = w.reshape(d, n // HEAD_DIM, 2, HEAD_DIM // 2)
    return jnp.stack([-w4[:, :, 1], w4[:, :, 0]], axis=2).reshape(d, n)


def _rope_tables(pos):
    half = HEAD_DIM // 2
    inv = ROPE_THETA ** (-jnp.arange(half, dtype=F32) / half)
    ang = pos.astype(F32)[:, None] * inv[None, :]
    cos = jnp.tile(jnp.cos(ang), (1, 2 * LANES // HEAD_DIM))
    sin = jnp.tile(jnp.sin(ang), (1, 2 * LANES // HEAD_DIM))
    return cos, sin


def _layer_params(l, P):
    w_in = P["w_in"][l]
    splits = np.cumsum([3 * D_MODEL, MIX_WIDTH, MIX_WIDTH, MIX_WIDTH, MIX_WIDTH, SSM_CONV_DIM, N_HEADS])
    w_gates, w_q, w_k, w_v, w_z, w_xbc, w_dt, w_rw = jnp.split(w_in, splits.tolist(), axis=1)
    pad = jnp.zeros((D_MODEL, COL_GATES - COL_DT - N_HEADS), F32)
    w_main = jnp.concatenate([w_rw, w_dt, pad, w_gates, w_xbc, w_z, w_v], axis=1).astype(BF16)
    w_qk = jnp.concatenate([w_q, w_k], axis=1)
    row = lambda a: a.reshape(1, -1)
    zeros_lora = jnp.zeros((DECAY_LORA, MIX_WIDTH), F32)
    head_of = np.arange(MIX_WIDTH) // HEAD_DIM
    return dict(
        norm1_g=row(P["norm1_g"][l]), w_main=w_main, w_qk=w_qk.astype(BF16), w_qk_rot=_rot_half_cols(w_qk).astype(BF16),
        ssm_conv_w=P["ssm_conv_w"][l], ssm_conv_b=row(P["ssm_conv_b"][l]),
        dtb_c=P["ssm_dt_bias"][l].reshape(-1, 1), dtb_r=row(P["ssm_dt_bias"][l]),
        alog_c=P["ssm_A_log"][l].reshape(-1, 1), alog_r=row(P["ssm_A_log"][l]),
        dskip=row(jnp.repeat(P["ssm_D"][l], HEAD_DIM)), ssm_norm_g=row(P["ssm_norm_g"][l]),
        rwkv_mu=row(P["rwkv_mu"][l]), rwkv_w0=row(P["rwkv_w0"][l]),
        w2_pad=jnp.concatenate([P["rwkv_w2"][l], zeros_lora], axis=0).astype(BF16),
        rwkv_a0=row(P["rwkv_a0"][l]),
        a2_pad=jnp.concatenate([zeros_lora, P["rwkv_a2"][l]], axis=0).astype(BF16),
        rwkv_g2=P["rwkv_g2"][l].astype(BF16),
        rwkv_k_k=row(P["rwkv_k_k"][l]), rwkv_k_a=row(P["rwkv_k_a"][l]), rwkv_r_k=row(P["rwkv_r_k"][l]),
        rwkv_ln_g=row(P["rwkv_ln_g"][l]), rwkv_ln_b=row(P["rwkv_ln_b"][l]),
        head_seg=jnp.asarray(head_of[:, None] == head_of[None, :], F32),
        w_br=jnp.stack([P["w_br_att"][l], P["w_br_ssm"][l], P["w_br_rwkv"][l]]).astype(BF16),
        w_out=P["w_out"][l].astype(BF16), norm2_g=row(P["norm2_g"][l]), w_up=P["w_up"][l].astype(BF16),
        ffn_conv_w=P["ffn_conv_w"][l], ffn_conv_b=row(P["ffn_conv_b"][l]), w_down=P["w_down"][l].astype(BF16),
    )


def _tail_rows(a, n_seq, seq, k):
    return a.reshape(n_seq, seq, -1)[:, seq - k:, :]


def _pad_history(h):
    n, k, c = h.shape
    return jnp.concatenate([jnp.zeros((n, SUBLANES - k, c), h.dtype), h], axis=1)


def _pack_pair_lanes(s):
    n, h, r, c = s.shape
    return s.reshape(n, h // 2, 2, r, c).transpose(0, 1, 3, 2, 4).reshape(n, h // 2, r, 2 * c)


def _unpack_pair_lanes(s):
    n, p, r, c2 = s.shape
    return s.reshape(n, p, r, 2, c2 // 2).transpose(0, 1, 3, 2, 4).reshape(n, 2 * p, r, c2 // 2)


def _pack_pair_diag(s):
    n, h, r, c = s.shape
    s = s.reshape(n, h // 2, 2, r, c)
    z = jnp.zeros_like(s[:, :, 0])
    top = jnp.concatenate([s[:, :, 0], z], axis=-1)
    bot = jnp.concatenate([z, s[:, :, 1]], axis=-1)
    return jnp.concatenate([top, bot], axis=-2)


def _unpack_pair_diag(s):
    n, p, r2, c2 = s.shape
    r, c = r2 // 2, c2 // 2
    return jnp.stack([s[:, :, :r, :c], s[:, :, r:, c:]], axis=2).reshape(n, 2 * p, r, c)


def _trunk_layer(x, n_seq, seq, prm, rope_tab, states, attention):
    m = x.shape[0]
    tm = min(512, m)
    fresh = states is None
    main = _norm_matmul(x, prm["norm1_g"], prm["w_main"], tm, MAIN_WIDTH // 4)
    qk = _rope_proj(x, prm["norm1_g"], prm["w_qk"], prm["w_qk_rot"], rope_tab[0], rope_tab[1], min(256, m))
    k_new = qk[:, MIX_WIDTH:]
    v_new = main[:, COL_V:COL_V + MIX_WIDTH]
    o_att = attention(qk, k_new, v_new)

    xbc_tail = _tail_rows(main[:, COL_XBC:COL_XBC + SSM_CONV_DIM], n_seq, seq, SSM_CONV - 1)
    dt_t = main[:, COL_DT:COL_DT + N_HEADS].reshape(n_seq, seq, N_HEADS).transpose(0, 2, 1)
    if fresh:
        conv0 = jnp.zeros((n_seq, SUBLANES, SSM_CONV_DIM), F32)
        ssm0 = jnp.zeros((n_seq, N_PAIRS, SSM_STATE, LANES), F32)
        shift0 = jnp.zeros((n_seq, SUBLANES, RWKV_SHIFT_DIM), F32)
        rwkv0 = jnp.zeros((n_seq, N_PAIRS, LANES, LANES), F32)
    else:
        conv0 = _pad_history(states["ssm_conv"])
        ssm0 = _pack_pair_lanes(states["ssm"])
        shift0 = _pad_history(states["rwkv_shift"][:, None, :])
        rwkv0 = _pack_pair_diag(states["rwkv"])
    o_ssm, ssm_fin = _ssd(main, dt_t, conv0, ssm0, prm, n_seq, seq, math.gcd(seq, SSM_CHUNK))
    o_rwkv, rwkv_fin = _rwkv(main, shift0, rwkv0, prm, n_seq, seq, min(seq, RWKV_CHUNK))
    x = _merge(o_att, o_ssm, o_rwkv, main, x, prm["w_br"], prm["w_out"], tm)

    if fresh:
        x, g_pre = _ffn(x, prm["norm2_g"], prm["w_up"], prm["ffn_conv_w"], prm["ffn_conv_b"], prm["w_down"],
                        n_seq, seq, min(256, seq))
    else:
        hist = states["ffn_conv"]
        p1 = jnp.repeat(hist[:, 1], seq, axis=0)
        p2 = jnp.repeat(hist[:, 0], seq, axis=0)
        x, g_pre = _ffn(x, prm["norm2_g"], prm["w_up"], prm["ffn_conv_w"], prm["ffn_conv_b"], prm["w_down"],
                        n_seq, seq, m, prev=(p1, p2))
    new_states = (
        k_new.reshape(n_seq, seq, N_HEADS, HEAD_DIM), v_new.reshape(n_seq, seq, N_HEADS, HEAD_DIM),
        _unpack_pair_lanes(ssm_fin), xbc_tail, _unpack_pair_diag(rwkv_fin),
        main[:, COL_RW:COL_RW + RWKV_SHIFT_DIM].reshape(n_seq, seq, -1)[:, -1],
        _tail_rows(g_pre, n_seq, seq, FFN_CONV - 1),
    )
    return x, new_states


def _prompt_attention(n_seq, seq):
    nb = seq // MOBA_BLOCK

    def attend(qk, k_new, v_new):
        kmean = _kmean(qk, n_seq, seq)
        bias = _moba_select(qk, kmean, n_seq, seq, MOBA_BLOCK)
        k_blocks = k_new.astype(BF16).reshape(n_seq, nb, MOBA_BLOCK, MIX_WIDTH)
        vt_blocks = v_new.astype(BF16).reshape(n_seq, nb, MOBA_BLOCK, MIX_WIDTH).transpose(0, 1, 3, 2)
        o_t = _moba_prompt(qk, k_blocks, vt_blocks, bias, n_seq, seq)
        return o_t.transpose(0, 2, 1).reshape(n_seq * seq, MIX_WIDTH)

    return attend


def _sample_attention(n_seq, seq, page_table, cache_k, cache_v):
    head_of = np.arange(MIX_WIDTH) // HEAD_DIM
    blockdiag = jnp.asarray(np.arange(N_HEADS)[:, None] == head_of[None, :], F32)

    def attend(qk, k_new, v_new):
        q = qk[:, :MIX_WIDTH].reshape(n_seq, 1, seq, MIX_WIDTH) * (HEAD_DIM ** -0.5)
        qbd = (q * blockdiag[None, :, None, :]).reshape(n_seq, N_HEADS * seq, MIX_WIDTH)
        o = _moba_sample(page_table, qbd, cache_k, cache_v,
                         k_new.reshape(n_seq, seq, MIX_WIDTH), v_new.reshape(n_seq, seq, MIX_WIDTH))
        return o.reshape(n_seq * seq, MIX_WIDTH)

    return attend


def kernel(x_prompt, x_sample, cache_k, cache_v, page_table, state_ssm, state_ssm_conv, state_rwkv, state_rwkv_shift, state_ffn_conv, norm1_g, w_in, ssm_conv_w, ssm_conv_b, ssm_dt_bias, ssm_A_log, ssm_D, ssm_norm_g, rwkv_mu, rwkv_w0, rwkv_w2, rwkv_a0, rwkv_a2, rwkv_g2, rwkv_k_k, rwkv_k_a, rwkv_r_k, rwkv_ln_g, rwkv_ln_b, w_br_att, w_br_ssm, w_br_rwkv, w_out, norm2_g, w_up, ffn_conv_w, ffn_conv_b, w_down, final_norm_g):
    P = dict(norm1_g=norm1_g, w_in=w_in, ssm_conv_w=ssm_conv_w, ssm_conv_b=ssm_conv_b, ssm_dt_bias=ssm_dt_bias,
             ssm_A_log=ssm_A_log, ssm_D=ssm_D, ssm_norm_g=ssm_norm_g, rwkv_mu=rwkv_mu, rwkv_w0=rwkv_w0,
             rwkv_w2=rwkv_w2, rwkv_a0=rwkv_a0, rwkv_a2=rwkv_a2, rwkv_g2=rwkv_g2, rwkv_k_k=rwkv_k_k,
             rwkv_k_a=rwkv_k_a, rwkv_r_k=rwkv_r_k, rwkv_ln_g=rwkv_ln_g, rwkv_ln_b=rwkv_ln_b, w_br_att=w_br_att,
             w_br_ssm=w_br_ssm, w_br_rwkv=w_br_rwkv, w_out=w_out, norm2_g=norm2_g, w_up=w_up,
             ffn_conv_w=ffn_conv_w, ffn_conv_b=ffn_conv_b, w_down=w_down)
    b_p, t_p, d = x_prompt.shape
    b_s, t_s, _ = x_sample.shape
    n_pages = page_table.shape[1]
    past_len = n_pages * cache_k.shape[2]
    depth = w_in.shape[0]
    xp = x_prompt.reshape(b_p * t_p, d)
    xs = x_sample.reshape(b_s * t_s, d)
    rope_p = _rope_tables(jnp.arange(t_p))
    rope_s = _rope_tables(jnp.tile(past_len + jnp.arange(t_s), b_s))
    att_p = _prompt_attention(b_p, t_p)
    new_p, new_s = [], []
    for l in range(depth):
        prm = _layer_params(l, P)
        xp, st_p = _trunk_layer(xp, b_p, t_p, prm, rope_p, None, att_p)
        states = dict(ssm=state_ssm[l], ssm_conv=state_ssm_conv[l], rwkv=state_rwkv[l],
                      rwkv_shift=state_rwkv_shift[l], ffn_conv=state_ffn_conv[l])
        att_s = _sample_attention(b_s, t_s, page_table,
                                  cache_k[l].reshape(cache_k.shape[1], PAGE_SIZE, MIX_WIDTH),
                                  cache_v[l].reshape(cache_v.shape[1], PAGE_SIZE, MIX_WIDTH))
        xs, st_s = _trunk_layer(xs, b_s, t_s, prm, rope_s, states, att_s)
        new_p.append(st_p)
        new_s.append(st_s)
    g_fin = final_norm_g.reshape(1, d)
    y_prompt = _final_norm(xp, g_fin, min(512, xp.shape[0])).reshape(b_p, t_p, d)
    y_sample = _final_norm(xs, g_fin, min(512, xs.shape[0])).reshape(b_s, t_s, d)
    stacked = lambda states, i: jnp.stack([s[i] for s in states])
    outs = [y_prompt, y_sample, stacked(new_p, 0), stacked(new_p, 1), stacked(new_s, 0), stacked(new_s, 1)]
    for i in range(2, 7):
        outs.append(stacked(new_p, i))
        outs.append(stacked(new_s, i))
    return tuple(outs)
```

```python
import functools
import math

import jax
import jax.numpy as jnp
import numpy as np
from jax import lax
from jax.experimental import pallas as pl
from jax.experimental.pallas import tpu as pltpu

F32 = jnp.float32
BF16 = jnp.bfloat16

D_MODEL = 1024
DEPTH = 4
PAGE_SIZE = 128
MIX_WIDTH = D_MODEL // 2
HEAD_DIM = 64
N_HEADS = MIX_WIDTH // HEAD_DIM
N_PAIRS = N_HEADS // 2
MOBA_BLOCK = 256
MOBA_TOPK = 3
KEY_SUB = 128
SAMPLE_BLOCKS_PER_STEP = 2
VT_ROWS = HEAD_DIM + 16
ROPE_THETA = 10000.0
SSM_GROUPS = 2
SSM_STATE = 128
SSM_CONV = 4
SSM_CHUNK = 128
SSM_CONV_DIM = MIX_WIDTH + 2 * SSM_GROUPS * SSM_STATE
DECAY_LORA = 64
ICLR_LORA = 64
GATE_LORA = 128
RWKV_SHIFT_DIM = 3 * MIX_WIDTH + DECAY_LORA + ICLR_LORA + GATE_LORA
RWKV_GN_EPS = 64e-5
RWKV_CHUNK = 128
D_FF = ((8 * D_MODEL // 3 + 127) // 128) * 128
FFN_CONV = 3
EPS = 1e-6
LANES = 128
SUBLANES = 8
NEG = -1e30
VMEM_LIMIT = 52 * 1024 * 1024

COL_RW = 0
COL_DT = RWKV_SHIFT_DIM
COL_GATES = 2048
COL_XBC = COL_GATES + 3 * D_MODEL
COL_Z = COL_XBC + SSM_CONV_DIM
COL_V = COL_Z + MIX_WIDTH
MAIN_WIDTH = COL_V + MIX_WIDTH


def _cparams(*sem):
    return pltpu.CompilerParams(dimension_semantics=sem, vmem_limit_bytes=VMEM_LIMIT)


def _rms(x, g):
    return x * lax.rsqrt(jnp.mean(x * x, axis=-1, keepdims=True) + EPS) * g


def _dot(a, b):
    return jnp.dot(a.astype(BF16), b.astype(BF16), preferred_element_type=F32)


def _dot_nt(a, b):
    return lax.dot_general(a.astype(BF16), b.astype(BF16), (((1,), (1,)), ((), ())), preferred_element_type=F32)


def _dot_tn(a, b):
    return lax.dot_general(a.astype(BF16), b.astype(BF16), (((0,), (0,)), ((), ())), preferred_element_type=F32)


def _dot_f32(a, b):
    return jnp.dot(a, b, preferred_element_type=F32, precision=lax.Precision.HIGHEST)


def _dot_split3(a, b01):
    hi = a.astype(BF16)
    r1 = a - hi.astype(F32)
    mid = r1.astype(BF16)
    lo = (r1 - mid.astype(F32)).astype(BF16)
    b = b01.astype(BF16)
    dot = lambda x: jnp.dot(x, b, preferred_element_type=F32)
    return dot(hi) + dot(mid) + dot(lo)


def _silu(x):
    return x * jax.nn.sigmoid(x)


def _softplus(x):
    return jnp.maximum(x, 0.0) + jnp.log(1.0 + jnp.exp(-jnp.abs(x)))


def _shift_rows(g, k, prevs, period):
    rows = g.shape[0]
    out = pltpu.roll(g, k, axis=0)
    tig = lax.broadcasted_iota(jnp.int32, (rows, 1), 0) & (period - 1)
    for t0 in range(k):
        out = jnp.where(tig == t0, prevs[k - t0 - 1], out)
    return out


def _norm_matmul_kernel(x_ref, g_ref, w_ref, o_ref):
    o_ref[...] = _dot(_rms(x_ref[...], g_ref[...]), w_ref[...])


def _norm_matmul(x, g, w, tm, tn):
    m, d = x.shape
    n = w.shape[1]
    return pl.pallas_call(
        _norm_matmul_kernel,
        grid=(n // tn, m // tm),
        in_specs=[pl.BlockSpec((tm, d), lambda j, i: (i, 0)),
                  pl.BlockSpec((1, d), lambda j, i: (0, 0)),
                  pl.BlockSpec((d, tn), lambda j, i: (0, j))],
        out_specs=pl.BlockSpec((tm, tn), lambda j, i: (i, j)),
        out_shape=jax.ShapeDtypeStruct((m, n), F32),
        compiler_params=_cparams("arbitrary", "arbitrary"),
        name="norm_matmul",
    )(x, g, w)


def _rope_proj_kernel(x_ref, g_ref, w_ref, wr_ref, cos_ref, sin_ref, o_ref):
    h = _rms(x_ref[...], g_ref[...]).astype(BF16)
    a = jnp.dot(h, w_ref[...], preferred_element_type=F32)
    b = jnp.dot(h, wr_ref[...], preferred_element_type=F32)
    reps = a.shape[1] // LANES
    cos = jnp.concatenate([cos_ref[...]] * reps, axis=1)
    sin = jnp.concatenate([sin_ref[...]] * reps, axis=1)
    o_ref[...] = a * cos + b * sin


def _rope_proj(x, g, w, wr, cos, sin, tm):
    m, d = x.shape
    n = w.shape[1]
    tab_blocks = cos.shape[0] // tm
    return pl.pallas_call(
        _rope_proj_kernel,
        grid=(m // tm,),
        in_specs=[pl.BlockSpec((tm, d), lambda i: (i, 0)),
                  pl.BlockSpec((1, d), lambda i: (0, 0)),
                  pl.BlockSpec((d, n), lambda i: (0, 0)),
                  pl.BlockSpec((d, n), lambda i: (0, 0)),
                  pl.BlockSpec((tm, LANES), lambda i: (i % tab_blocks, 0)),
                  pl.BlockSpec((tm, LANES), lambda i: (i % tab_blocks, 0))],
        out_specs=pl.BlockSpec((tm, n), lambda i: (i, 0)),
        out_shape=jax.ShapeDtypeStruct((m, n), F32),
        compiler_params=_cparams("arbitrary"),
        name="rope_proj",
    )(x, g, w, wr, cos, sin)


def _kmean_kernel(k_ref, o_ref):
    k = k_ref[...]
    o_ref[...] = jnp.mean(k.reshape(SUBLANES, MOBA_BLOCK, k.shape[-1]), axis=1)


def _kmean(k2d, n_seq, seq):
    nb = seq // MOBA_BLOCK
    w = k2d.shape[1] // 2
    rows = SUBLANES * MOBA_BLOCK
    return pl.pallas_call(
        _kmean_kernel,
        grid=(n_seq * nb // SUBLANES,),
        in_specs=[pl.BlockSpec((rows, w), lambda i: (i, 1))],
        out_specs=pl.BlockSpec((SUBLANES, w), lambda i: (i, 0)),
        out_shape=jax.ShapeDtypeStruct((n_seq * nb, w), F32),
        compiler_params=_cparams("arbitrary"),
        name="moba_kmean",
    )(k2d).reshape(n_seq, nb, w)


def _head_mask(pair_vals, hl):
    lane = lax.broadcasted_iota(jnp.int32, pair_vals.shape, 1)
    return jnp.where((lane >= hl * HEAD_DIM) & (lane < (hl + 1) * HEAD_DIM), pair_vals, jnp.zeros_like(pair_vals))


def _moba_select_kernel(q_ref, km_ref, o_ref, *, nb):
    tq = q_ref.shape[0]
    q0 = pl.program_id(1) * tq
    qpos = q0 + lax.broadcasted_iota(jnp.int32, (1, tq), 1)
    own = qpos // MOBA_BLOCK
    kb = lax.broadcasted_iota(jnp.int32, (nb, tq), 0)
    kbf = kb.astype(F32)
    valid = kb < own
    for h in range(N_HEADS):
        p, hl = divmod(h, 2)
        lanes = slice(p * LANES, (p + 1) * LANES)
        km = _head_mask(km_ref[:, lanes], hl)
        gate = lax.dot_general(km, q_ref[:, lanes], (((1,), (1,)), ((), ())),
                               preferred_element_type=F32, precision=lax.Precision.HIGHEST)
        g = jnp.where(valid, gate, -jnp.inf)
        sel = jnp.zeros((nb, tq), jnp.bool_)
        for _ in range(MOBA_TOPK):
            m = jnp.max(g, axis=0, keepdims=True)
            first = jnp.min(jnp.where(g == m, kbf, float(nb)), axis=0, keepdims=True)
            hit = (kbf == first) & (m > -jnp.inf)
            sel = sel | hit
            g = jnp.where(hit, -jnp.inf, g)
        o_ref[h] = jnp.where(sel, 0.0, NEG)


def _moba_select(qk2d, kmean, n_seq, seq, tq):
    nb = seq // MOBA_BLOCK
    w = qk2d.shape[1] // 2
    tiles = seq // tq
    return pl.pallas_call(
        functools.partial(_moba_select_kernel, nb=nb),
        grid=(n_seq, tiles),
        in_specs=[pl.BlockSpec((tq, w), lambda b, i: (b * tiles + i, 0)),
                  pl.BlockSpec((None, nb, w), lambda b, i: (b, 0, 0))],
        out_specs=pl.BlockSpec((None, N_HEADS, nb, tq), lambda b, i: (b, 0, 0, i)),
        out_shape=jax.ShapeDtypeStruct((n_seq, N_HEADS, nb, seq), F32),
        compiler_params=_cparams("arbitrary", "arbitrary"),
        name="moba_select",
    )(qk2d, kmean)


def _moba_prompt_kernel(q_ref, k_ref, vt_ref, bias_ref, o_ref, qm_scr, m_scr, acc_scr, s_scr):
    tq = q_ref.shape[0]
    j = pl.program_id(1)
    scale = HEAD_DIM ** -0.5 * math.log2(math.e)
    key_i = lax.broadcasted_iota(jnp.int32, (MOBA_BLOCK, tq), 0)
    qry_i = lax.broadcasted_iota(jnp.int32, (MOBA_BLOCK, tq), 1)
    causal = key_i <= qry_i
    for h in range(N_HEADS):
        lanes = slice((h // 2) * LANES, (h // 2 + 1) * LANES)
        rows = slice(h * VT_ROWS, (h + 1) * VT_ROWS)
        qm = _head_mask((q_ref[:, lanes] * scale).astype(BF16), h % 2)
        qm_scr[h] = qm
        s = jnp.where(causal, _dot_nt(k_ref[j, :, lanes], qm), NEG)
        m = jnp.max(s, axis=0, keepdims=True)
        m_scr[h:h + 1, :] = m
        acc_scr[rows, :] = _dot(vt_ref[j, rows, :], jnp.exp2(s - m))
        s_scr[0, h] = _dot_nt(k_ref[0, 0:KEY_SUB, lanes], qm)

    def body(kb, carry):
        for sub in range(MOBA_BLOCK // KEY_SUB):
            cur = sub % 2
            if sub + 1 < MOBA_BLOCK // KEY_SUB:
                kn, r0 = kb, (sub + 1) * KEY_SUB
            else:
                kn, r0 = jnp.minimum(kb + 1, j), 0
            for h in range(N_HEADS):
                lanes = slice((h // 2) * LANES, (h // 2 + 1) * LANES)
                s_scr[1 - cur, h] = _dot_nt(k_ref[kn, r0:r0 + KEY_SUB, lanes], qm_scr[h])
            for h in range(N_HEADS):
                rows = slice(h * VT_ROWS, (h + 1) * VT_ROWS)
                s = s_scr[cur, h] + bias_ref[h, pl.ds(kb, 1), :]
                m = m_scr[h:h + 1, :]
                m_new = jnp.maximum(m, jnp.max(s, axis=0, keepdims=True))
                m_scr[h:h + 1, :] = m_new
                acc_scr[rows, :] = (acc_scr[rows, :] * jnp.exp2(m - m_new)
                                    + _dot(vt_ref[kb, rows, sub * KEY_SUB:(sub + 1) * KEY_SUB], jnp.exp2(s - m_new)))
        return carry

    lax.fori_loop(0, j, body, 0)
    for h in range(N_HEADS):
        r0 = h * VT_ROWS
        o_ref[h * HEAD_DIM:(h + 1) * HEAD_DIM, :] = (acc_scr[r0:r0 + HEAD_DIM, :]
                                                     / acc_scr[r0 + HEAD_DIM:r0 + HEAD_DIM + 1, :])


def _moba_prompt(qk2d, k_blocks, vt_blocks, bias, n_seq, seq):
    nb = seq // MOBA_BLOCK
    w = MIX_WIDTH
    tq = MOBA_BLOCK
    return pl.pallas_call(
        _moba_prompt_kernel,
        grid=(n_seq, nb),
        in_specs=[pl.BlockSpec((tq, w), lambda b, i: (b * nb + i, 0)),
                  pl.BlockSpec((None, nb, MOBA_BLOCK, w), lambda b, i: (b, 0, 0, 0)),
                  pl.BlockSpec((None, nb, N_HEADS * VT_ROWS, MOBA_BLOCK), lambda b, i: (b, 0, 0, 0)),
                  pl.BlockSpec((None, N_HEADS, nb, tq), lambda b, i: (b, 0, 0, i))],
        out_specs=pl.BlockSpec((None, w, tq), lambda b, i: (b, 0, i)),
        out_shape=jax.ShapeDtypeStruct((n_seq, w, seq), F32),
        scratch_shapes=[pltpu.VMEM((N_HEADS, tq, LANES), BF16), pltpu.VMEM((N_HEADS, tq), F32),
                        pltpu.VMEM((N_HEADS * VT_ROWS, tq), F32),
                        pltpu.VMEM((2, N_HEADS, KEY_SUB, tq), F32)],
        compiler_params=_cparams("arbitrary", "arbitrary"),
        name="moba_prompt",
    )(qk2d, k_blocks, vt_blocks, bias)


def _moba_sample_kernel(pt_ref, qbd_ref, *refs, nb, blocks_per_step):
    del pt_ref
    n_pages = 2 * blocks_per_step
    k_refs = refs[:n_pages]
    v_refs = refs[n_pages:2 * n_pages]
    kn_ref, vn_ref, o_ref, r_scr, m_scr, l_scr, g_scr = refs[2 * n_pages:]
    step = pl.program_id(1)
    nq = qbd_ref.shape[0]
    lane = lax.broadcasted_iota(jnp.int32, (nq, LANES), 1)

    @pl.when(step == 0)
    def _():
        m_scr[...] = jnp.full(m_scr.shape, NEG, F32)
        l_scr[...] = jnp.zeros(l_scr.shape, F32)
        g_scr[...] = jnp.full(g_scr.shape, -jnp.inf, F32)

    qbd = qbd_ref[...].astype(BF16)
    m_all, l_all, g_all = m_scr[...], l_scr[...], g_scr[...]
    for i in range(blocks_per_step):
        b = step * blocks_per_step + i
        kt = jnp.concatenate([k_refs[2 * i][...], k_refs[2 * i + 1][...]], axis=1)
        vt = jnp.concatenate([v_refs[2 * i][...], v_refs[2 * i + 1][...]], axis=1)
        s = _dot(qbd, kt)
        gate = jnp.sum(s, axis=-1, keepdims=True)
        m_b = jnp.max(s, axis=-1, keepdims=True)
        p = jnp.exp(s - m_b)
        l_b = jnp.sum(p, axis=-1, keepdims=True)
        r_scr[b] = _dot_nt(p, vt)
        col = lane == b
        m_all = jnp.where(col, m_b, m_all)
        l_all = jnp.where(col, l_b, l_all)
        g_all = jnp.where(col, gate, g_all)
    m_scr[...] = m_all
    l_scr[...] = l_all
    g_scr[...] = g_all

    @pl.when(step == nb // blocks_per_step - 1)
    def _():
        lane_f = lane.astype(F32)
        g = g_scr[...]
        sel = jnp.zeros((nq, LANES), jnp.bool_)
        for _ in range(MOBA_TOPK):
            mx = jnp.max(g, axis=-1, keepdims=True)
            first = jnp.min(jnp.where(g == mx, lane_f, float(LANES)), axis=-1, keepdims=True)
            hit = (lane_f == first) & (mx > -jnp.inf)
            sel = sel | hit
            g = jnp.where(hit, -jnp.inf, g)
        n_new = kn_ref.shape[0]
        s_own = _dot_nt(qbd, kn_ref[...])
        q_t = lax.broadcasted_iota(jnp.int32, (nq, n_new), 0) & (n_new - 1)
        k_t = lax.broadcasted_iota(jnp.int32, (nq, n_new), 1)
        s_own = jnp.where(k_t <= q_t, s_own, NEG)
        m_all = m_scr[...]
        m_tot = jnp.maximum(jnp.max(jnp.where(sel, m_all, NEG), axis=-1, keepdims=True),
                            jnp.max(s_own, axis=-1, keepdims=True))
        w_sel = jnp.where(sel, jnp.exp(m_all - m_tot), 0.0)
        p_own = jnp.exp(s_own - m_tot)
        l_tot = jnp.sum(w_sel * l_scr[...], axis=-1, keepdims=True) + jnp.sum(p_own, axis=-1, keepdims=True)
        acc = _dot(p_own, vn_ref[...])
        for bb in range(nb):
            acc = acc + w_sel[:, bb:bb + 1] * r_scr[bb]
        acc = acc / l_tot
        lane_w = lax.broadcasted_iota(jnp.int32, (n_new, acc.shape[1]), 1) // HEAD_DIM
        out = jnp.zeros((n_new, acc.shape[1]), F32)
        for h in range(N_HEADS):
            out = out + jnp.where(lane_w == h, acc[h * n_new:(h + 1) * n_new, :], 0.0)
        o_ref[...] = out


def _moba_sample(page_table, qbd, cache_kt, cache_vt, layer, k_new, v_new):
    n_seq, nq, w = qbd.shape
    t_new = k_new.shape[1]
    n_pages = page_table.shape[1]
    nb = n_pages * PAGE_SIZE // MOBA_BLOCK
    bps = SAMPLE_BLOCKS_PER_STEP
    assert MOBA_BLOCK == 2 * PAGE_SIZE and nb <= LANES and nb % bps == 0
    page = lambda off: pl.BlockSpec((None, None, w, PAGE_SIZE),
                                    lambda s, b, pt: (layer, pt[s * n_pages + 2 * bps * b + off], 0, 0))
    pages = [page(off) for off in range(2 * bps)]
    grid_spec = pltpu.PrefetchScalarGridSpec(
        num_scalar_prefetch=1,
        grid=(n_seq, nb // bps),
        in_specs=[pl.BlockSpec((None, nq, w), lambda s, b, pt: (s, 0, 0))] + pages + pages
                 + [pl.BlockSpec((None, t_new, w), lambda s, b, pt: (s, 0, 0)),
                    pl.BlockSpec((None, t_new, w), lambda s, b, pt: (s, 0, 0))],
        out_specs=pl.BlockSpec((None, t_new, w), lambda s, b, pt: (s, 0, 0)),
        scratch_shapes=[pltpu.VMEM((nb, nq, w), F32), pltpu.VMEM((nq, LANES), F32),
                        pltpu.VMEM((nq, LANES), F32), pltpu.VMEM((nq, LANES), F32)],
    )
    return pl.pallas_call(
        functools.partial(_moba_sample_kernel, nb=nb, blocks_per_step=bps),
        grid_spec=grid_spec,
        out_shape=jax.ShapeDtypeStruct((n_seq, t_new, w), F32),
        compiler_params=_cparams("arbitrary", "arbitrary"),
        name="moba_sample",
    )(page_table.reshape(-1), qbd, *([cache_kt] * (2 * bps)), *([cache_vt] * (2 * bps)), k_new, v_new)


def _ssd_kernel(xbc_ref, z_ref, dtc_ref, dtt_ref, conv0_ref, s0_ref, cw_ref, cb_ref, dtb_c_ref, dtb_r_ref,
                a_c_ref, a_r_ref, dskip_ref, ng_ref, y_ref, sfin_ref, carry_scr, s_scr):
    c = pl.program_id(1)
    q = xbc_ref.shape[0]

    @pl.when(c == 0)
    def _():
        carry_scr[...] = conv0_ref[...]
        s_scr[...] = s0_ref[...]

    xbc = xbc_ref[...]
    carry = carry_scr[...]
    prevs = [carry[SUBLANES - i:SUBLANES - i + 1, :] for i in range(1, SSM_CONV)]
    conv = cb_ref[...] + cw_ref[SSM_CONV - 1:SSM_CONV, :] * xbc
    for k in range(1, SSM_CONV):
        conv = conv + cw_ref[SSM_CONV - 1 - k:SSM_CONV - k, :] * _shift_rows(xbc, k, prevs, q)
    carry_scr[...] = xbc[q - SUBLANES:, :]
    conv = _silu(conv)
    xs = conv[:, :MIX_WIDTH]
    gw = SSM_STATE
    bmat = [conv[:, MIX_WIDTH + g * gw:MIX_WIDTH + (g + 1) * gw] for g in range(SSM_GROUPS)]
    cmat = [conv[:, MIX_WIDTH + (SSM_GROUPS + g) * gw:MIX_WIDTH + (SSM_GROUPS + g + 1) * gw] for g in range(SSM_GROUPS)]

    dt_c = _softplus(dtc_ref[:, :N_HEADS] + dtb_r_ref[...])
    dt_t = _softplus(dtt_ref[...] + dtb_c_ref[...])
    a_r = -jnp.exp(a_r_ref[...])
    a_c = -jnp.exp(a_c_ref[...])
    ri = lax.broadcasted_iota(jnp.int32, (q, q), 0)
    ci = lax.broadcasted_iota(jnp.int32, (q, q), 1)
    lower = (ri >= ci).astype(F32)
    cum_c = _dot_f32(lower, dt_c * a_r)
    cum_t = _dot_f32(dt_t * a_c, (ri <= ci).astype(F32))
    causal = ri >= ci
    cb = [_dot_nt(cmat[g], bmat[g]) for g in range(SSM_GROUPS)]
    heads_per_group = N_HEADS // SSM_GROUPS
    lane = lax.broadcasted_iota(jnp.int32, (1, LANES), 1)
    first_half = lane < HEAD_DIM

    for p in range(N_PAIRS):
        lanes = slice(p * LANES, (p + 1) * LANES)
        xs_p = xs[:, lanes]
        s_in = s_scr[p]
        y_h, sc_h, cd_h = [], [], []
        for hl in range(2):
            h = 2 * p + hl
            g = h // heads_per_group
            cc = cum_c[:, h:h + 1]
            ct = cum_t[h:h + 1, :]
            lmat = jnp.where(causal, jnp.exp(jnp.where(causal, cc - ct, 0.0)), 0.0)
            y_diag = _dot(cb[g] * lmat * dt_t[h:h + 1, :], xs_p)
            y_off = _dot(cmat[g] * jnp.exp(cc), s_in)
            y_h.append(y_diag + y_off)
            dec_end = jnp.exp(cc[q - 1:q, :] - cc)
            sc_h.append(_dot_tn(bmat[g] * (dec_end * dt_c[:, h:h + 1]), xs_p))
            cd_h.append(jnp.exp(cc[q - 1:q, :]))
        y_ref[:, lanes] = jnp.where(first_half, y_h[0], y_h[1])
        s_scr[p] = (s_in * jnp.where(first_half, cd_h[0], cd_h[1])
                    + jnp.where(first_half, sc_h[0], sc_h[1]))

    y = (y_ref[...] + dskip_ref[...] * xs) * _silu(z_ref[...])
    gwid = MIX_WIDTH // SSM_GROUPS
    for g in range(SSM_GROUPS):
        cols = slice(g * gwid, (g + 1) * gwid)
        y_ref[:, cols] = _rms(y[:, cols], ng_ref[:, cols])
    sfin_ref[...] = s_scr[...]


def _ssd(main, dt_t, conv0, s0, prm, n_seq, seq, chunk):
    nc = seq // chunk
    xw = SSM_CONV_DIM
    row = lambda b, c: b * nc + c
    const2 = lambda b, c: (0, 0)
    return pl.pallas_call(
        _ssd_kernel,
        grid=(n_seq, nc),
        in_specs=[pl.BlockSpec((chunk, xw), lambda b, c: (row(b, c), COL_XBC // xw)),
                  pl.BlockSpec((chunk, MIX_WIDTH), lambda b, c: (row(b, c), COL_Z // MIX_WIDTH)),
                  pl.BlockSpec((chunk, LANES), lambda b, c: (row(b, c), COL_DT // LANES)),
                  pl.BlockSpec((None, N_HEADS, chunk), lambda b, c: (b, 0, c)),
                  pl.BlockSpec((None, SUBLANES, xw), lambda b, c: (b, 0, 0)),
                  pl.BlockSpec((None, N_PAIRS, SSM_STATE, LANES), lambda b, c: (b, 0, 0, 0)),
                  pl.BlockSpec((SSM_CONV, xw), const2),
                  pl.BlockSpec((1, xw), const2),
                  pl.BlockSpec((N_HEADS, 1), const2),
                  pl.BlockSpec((1, N_HEADS), const2),
                  pl.BlockSpec((N_HEADS, 1), const2),
                  pl.BlockSpec((1, N_HEADS), const2),
                  pl.BlockSpec((1, MIX_WIDTH), const2),
                  pl.BlockSpec((1, MIX_WIDTH), const2)],
        out_specs=[pl.BlockSpec((chunk, MIX_WIDTH), lambda b, c: (row(b, c), 0)),
                   pl.BlockSpec((None, N_PAIRS, SSM_STATE, LANES), lambda b, c: (b, 0, 0, 0))],
        out_shape=[jax.ShapeDtypeStruct((n_seq * seq, MIX_WIDTH), F32),
                   jax.ShapeDtypeStruct((n_seq, N_PAIRS, SSM_STATE, LANES), F32)],
        scratch_shapes=[pltpu.VMEM((SUBLANES, xw), F32), pltpu.VMEM((N_PAIRS, SSM_STATE, LANES), F32)],
        compiler_params=_cparams("arbitrary", "arbitrary"),
        name="ssd",
    )(main, main, main, dt_t, conv0, s0, prm["ssm_conv_w"], prm["ssm_conv_b"], prm["dtb_c"], prm["dtb_r"],
      prm["alog_c"], prm["alog_r"], prm["dskip"], prm["ssm_norm_g"])


def _rwkv_kernel(rw_ref, shift0_ref, s0_ref, mu_ref, w0_ref, w2_ref, a0_ref, a2_ref, g2_ref, kk_ref, ka_ref,
                 rk_ref, lng_ref, lnb_ref, seg_ref, y_ref, sfin_ref, carry_scr, s_scr):
    c = pl.program_id(1)
    n_t = rw_ref.shape[0]
    w = MIX_WIDTH

    @pl.when(c == 0)
    def _():
        carry_scr[...] = shift0_ref[...]
        s_scr[...] = s0_ref[...]

    p_in = rw_ref[...]
    p_prev = _shift_rows(p_in, 1, [carry_scr[SUBLANES - 1:SUBLANES, :]], n_t)
    carry_scr[...] = p_in[n_t - SUBLANES:, :]
    xs = p_in + (p_prev - p_in) * mu_ref[...]
    r = xs[:, 0:w]
    k = xs[:, w:2 * w]
    v = xs[:, 2 * w:3 * w]
    lora_in = xs[:, 3 * w:3 * w + LANES]
    gate_in = xs[:, 3 * w + LANES:]
    w_log = -_softplus(-(w0_ref[...] + _dot(jnp.tanh(lora_in), w2_ref[...]))) - 0.5
    logw = -jnp.exp(w_log)
    a = jax.nn.sigmoid(a0_ref[...] + _dot(lora_in, a2_ref[...]))
    g = _dot(jax.nn.sigmoid(gate_in), g2_ref[...])
    seg = seg_ref[...]
    kk = k * kk_ref[...]
    kk = kk * lax.rsqrt(jnp.maximum(_dot_split3(kk * kk, seg), 1e-12))
    k = k * (1.0 + (a - 1.0) * ka_ref[...])
    bonus = _dot_split3(r * k * rk_ref[...], seg) * v

    ri = lax.broadcasted_iota(jnp.int32, (n_t, n_t), 0)
    ci = lax.broadcasted_iota(jnp.int32, (n_t, n_t), 1)
    strict = ri > ci
    incl = ri >= ci
    cl = _dot_f32(incl.astype(F32), logw)
    mid = n_t // 2 - 1
    cl_mid = cl[mid:mid + 1, :]
    cm = cl - cl_mid
    p_inv = jnp.exp(-cm)
    a_t = -kk * jnp.exp(cm - logw)
    b_t = kk * a * p_inv
    k_t = k * p_inv
    r_t = r * jnp.exp(cm)
    to_mid = jnp.exp(cl_mid)
    mid_to_end = jnp.exp(cm[n_t - 1:n_t, :])
    p_end = jnp.exp(cl[n_t - 1:n_t, :])
    n_dbl = max(int(math.log2(n_t)) - 1, 0)
    first_half = lax.broadcasted_iota(jnp.int32, (1, LANES), 1) < HEAD_DIM
    heads = range(N_HEADS)
    pair_lanes = [slice((h // 2) * LANES, (h // 2 + 1) * LANES) for h in heads]
    am = [_head_mask(a_t[:, pair_lanes[h]], h % 2) for h in heads]
    rm = [_head_mask(r_t[:, pair_lanes[h]], h % 2) for h in heads]
    bm = [_head_mask(b_t[:, pair_lanes[h]], h % 2) for h in heads]
    km = [_head_mask(k_t[:, pair_lanes[h]], h % 2) for h in heads]
    vp = [v[:, pair_lanes[h]] for h in heads]
    gram = [_dot_nt(jnp.concatenate([am[h], rm[h]], axis=0), jnp.concatenate([bm[h], km[h]], axis=0))
            for h in heads]
    l_pow = [jnp.where(strict, gram[h][:n_t, :n_t], 0.0) for h in heads]
    l_ak = [jnp.where(strict, gram[h][:n_t, n_t:], 0.0) for h in heads]
    m_rb = [jnp.where(incl, gram[h][n_t:, :n_t], 0.0) for h in heads]
    m_rk = [jnp.where(incl, gram[h][n_t:, n_t:], 0.0) for h in heads]
    wu = [jnp.concatenate([am[h], _dot(l_ak[h], vp[h])], axis=1) for h in heads]
    wu = [wu[h] + _dot(l_pow[h], wu[h]) for h in heads]
    for _ in range(n_dbl):
        l_pow = [_dot(l_pow[h], l_pow[h]) for h in heads]
        wu = [wu[h] + _dot(l_pow[h], wu[h]) for h in heads]
    mw = [_dot(m_rb[h], wu[h]) for h in heads]
    mv = [_dot(m_rk[h], vp[h]) for h in heads]
    bw = [_dot_tn(wu[h], bm[h]) for h in heads]
    top = lax.broadcasted_iota(jnp.int32, (LANES, LANES), 0) < HEAD_DIM
    diag = top == (lax.broadcasted_iota(jnp.int32, (LANES, LANES), 1) < HEAD_DIM)
    for p in range(N_PAIRS):
        h0, h1 = 2 * p, 2 * p + 1
        lanes = pair_lanes[h0]
        q_hat = rm[h0] + mw[h0][:, :LANES] + rm[h1] + mw[h1][:, :LANES]
        y_hat = jnp.where(first_half, mw[h0][:, LANES:] + mv[h0], mw[h1][:, LANES:] + mv[h1])
        s_in = s_scr[p]
        s_mid = s_in * to_mid[:, lanes]
        y_ref[:, lanes] = _dot_nt(q_hat, s_mid) + y_hat
        vk = _dot_tn(vp[h0], k_t[:, lanes])
        h_t = jnp.where(diag, jnp.where(top, bw[h0][LANES:, :], bw[h1][LANES:, :]) + vk, 0.0)
        wb = bw[h0][:LANES, :] + bw[h1][:LANES, :]
        s_scr[p] = s_in * p_end[:, lanes] + (_dot(s_mid, wb) + h_t) * mid_to_end[:, lanes]

    y = y_ref[...]
    mean = _dot_split3(y, seg) * (1.0 / HEAD_DIM)
    yc = y - mean
    var = _dot_split3(yc * yc, seg) * (1.0 / HEAD_DIM)
    yn = yc * lax.rsqrt(var + RWKV_GN_EPS) * lng_ref[...] + lnb_ref[...]
    y_ref[...] = (yn + bonus) * g
    sfin_ref[...] = s_scr[...]


def _rwkv(main, shift0, s0, prm, n_seq, seq, chunk):
    nc = seq // chunk
    rw = RWKV_SHIFT_DIM
    w = MIX_WIDTH
    const2 = lambda b, c: (0, 0)
    vec = pl.BlockSpec((1, w), const2)
    return pl.pallas_call(
        _rwkv_kernel,
        grid=(n_seq, nc),
        in_specs=[pl.BlockSpec((chunk, rw), lambda b, c: (b * nc + c, COL_RW // rw)),
                  pl.BlockSpec((None, SUBLANES, rw), lambda b, c: (b, 0, 0)),
                  pl.BlockSpec((None, N_PAIRS, LANES, LANES), lambda b, c: (b, 0, 0, 0)),
                  pl.BlockSpec((1, rw), const2),
                  vec, pl.BlockSpec((LANES, w), const2),
                  vec, pl.BlockSpec((LANES, w), const2),
                  pl.BlockSpec((GATE_LORA, w), const2),
                  vec, vec, vec, vec, vec,
                  pl.BlockSpec((w, w), const2)],
        out_specs=[pl.BlockSpec((chunk, w), lambda b, c: (b * nc + c, 0)),
                   pl.BlockSpec((None, N_PAIRS, LANES, LANES), lambda b, c: (b, 0, 0, 0))],
        out_shape=[jax.ShapeDtypeStruct((n_seq * seq, w), F32),
                   jax.ShapeDtypeStruct((n_seq, N_PAIRS, LANES, LANES), F32)],
        scratch_shapes=[pltpu.VMEM((SUBLANES, rw), F32), pltpu.VMEM((N_PAIRS, LANES, LANES), F32)],
        compiler_params=_cparams("arbitrary", "arbitrary"),
        name="rwkv7",
    )(main, shift0, s0, prm["rwkv_mu"], prm["rwkv_w0"], prm["w2_pad"], prm["rwkv_a0"], prm["a2_pad"],
      prm["rwkv_g2"], prm["rwkv_k_k"], prm["rwkv_k_a"], prm["rwkv_r_k"], prm["rwkv_ln_g"], prm["rwkv_ln_b"],
      prm["head_seg"])


def _merge_kernel(oa_ref, os_ref, or_ref, ga_ref, gs_ref, gr_ref, x_ref, wbr_ref, wo_ref, o_ref):
    merged = (jax.nn.sigmoid(ga_ref[...]) * _dot(oa_ref[...], wbr_ref[0])
              + jax.nn.sigmoid(gs_ref[...]) * _dot(os_ref[...], wbr_ref[1])
              + jax.nn.sigmoid(gr_ref[...]) * _dot(or_ref[...], wbr_ref[2]))
    o_ref[...] = x_ref[...] + _dot(merged, wo_ref[...])


def _merge(o_att, o_ssm, o_rwkv, main, x, w_br, w_out, tm):
    m, d = x.shape
    w = MIX_WIDTH
    act = pl.BlockSpec((tm, w), lambda i: (i, 0))
    gate = lambda b: pl.BlockSpec((tm, d), lambda i: (i, COL_GATES // d + b))
    return pl.pallas_call(
        _merge_kernel,
        grid=(m // tm,),
        in_specs=[act, act, act, gate(0), gate(1), gate(2),
                  pl.BlockSpec((tm, d), lambda i: (i, 0)),
                  pl.BlockSpec((3, w, d), lambda i: (0, 0, 0)),
                  pl.BlockSpec((d, d), lambda i: (0, 0))],
        out_specs=pl.BlockSpec((tm, d), lambda i: (i, 0)),
        out_shape=jax.ShapeDtypeStruct((m, d), F32),
        compiler_params=_cparams("arbitrary"),
        name="merge_out",
    )(o_att, o_ssm, o_rwkv, main, main, main, x, w_br, w_out)


def _ffn_kernel(x_ref, g_ref, wup_ref, cw_ref, cb_ref, wdn_ref, *rest, period, carried):
    if carried:
        o_ref, gp_ref, carry_scr = rest
    else:
        p1_ref, p2_ref, o_ref, gp_ref = rest
    x = x_ref[...]
    up = _dot(_rms(x, g_ref[...]), wup_ref[...])
    g_pre = up[:, :D_FF]
    u = up[:, D_FF:]
    gp_ref[...] = g_pre
    tm = x.shape[0]
    if carried:
        @pl.when(pl.program_id(1) == 0)
        def _():
            carry_scr[...] = jnp.zeros(carry_scr.shape, F32)
        prevs = [carry_scr[SUBLANES - i:SUBLANES - i + 1, :] for i in range(1, FFN_CONV)]
    else:
        prevs = [p1_ref[...], p2_ref[...]]
    conv = cb_ref[...] + cw_ref[FFN_CONV - 1:FFN_CONV, :] * g_pre
    for k in range(1, FFN_CONV):
        conv = conv + cw_ref[FFN_CONV - 1 - k:FFN_CONV - k, :] * _shift_rows(g_pre, k, prevs, period)
    if carried:
        carry_scr[...] = g_pre[tm - SUBLANES:, :]
    o_ref[...] = x + _dot(_silu(conv) * u, wdn_ref[...])


def _ffn(x, g, w_up, conv_w, conv_b, w_down, n_seq, seq, tm, prev=None):
    m, d = x.shape
    carried = prev is None
    const = lambda *_: (0, 0)
    if carried:
        tiles = seq // tm
        grid = (n_seq, tiles)
        row = lambda b, i: (b * tiles + i, 0)
        period = tm
        extra_in, extra_args = [], []
        scratch = [pltpu.VMEM((SUBLANES, D_FF), F32)]
    else:
        assert m == tm
        grid = (1, 1)
        row = lambda b, i: (0, 0)
        period = seq
        extra_in = [pl.BlockSpec((tm, D_FF), row)] * 2
        extra_args = list(prev)
        scratch = []
    return pl.pallas_call(
        functools.partial(_ffn_kernel, period=period, carried=carried),
        grid=grid,
        in_specs=[pl.BlockSpec((tm, d), row), pl.BlockSpec((1, d), const),
                  pl.BlockSpec((d, 2 * D_FF), const), pl.BlockSpec((FFN_CONV, D_FF), const),
                  pl.BlockSpec((1, D_FF), const), pl.BlockSpec((D_FF, d), const)] + extra_in,
        out_specs=[pl.BlockSpec((tm, d), row), pl.BlockSpec((tm, D_FF), row)],
        out_shape=[jax.ShapeDtypeStruct((m, d), F32), jax.ShapeDtypeStruct((m, D_FF), F32)],
        scratch_shapes=scratch,
        compiler_params=_cparams("arbitrary", "arbitrary"),
        name="conv_ffn",
    )(x, g, w_up, conv_w, conv_b, w_down, *extra_args)


def _final_norm_kernel(x_ref, g_ref, o_ref):
    o_ref[...] = _rms(x_ref[...], g_ref[...])


def _final_norm(x, g, tm):
    m, d = x.shape
    return pl.pallas_call(
        _final_norm_kernel,
        grid=(m // tm,),
        in_specs=[pl.BlockSpec((tm, d), lambda i: (i, 0)), pl.BlockSpec((1, d), lambda i: (0, 0))],
        out_specs=pl.BlockSpec((tm, d), lambda i: (i, 0)),
        out_shape=jax.ShapeDtypeStruct((m, d), F32),
        compiler_params=_cparams("arbitrary"),
        name="final_norm",
    )(x, g)


def _rot_half_cols(w):
    d, n = w.shape
    w4 = w.reshape(d, n // HEAD_DIM, 2, HEAD_DIM // 2)
    return jnp.stack([-w4[:, :, 1], w4[:, :, 0]], axis=2).reshape(d, n)


def _rope_tables(pos):
    half = HEAD_DIM // 2
    inv = ROPE_THETA ** (-jnp.arange(half, dtype=F32) / half)
    ang = pos.astype(F32)[:, None] * inv[None, :]
    cos = jnp.tile(jnp.cos(ang), (1, 2 * LANES // HEAD_DIM))
    sin = jnp.tile(jnp.sin(ang), (1, 2 * LANES // HEAD_DIM))
    return cos, sin


def _layer_params(l, P):
    w_in = P["w_in"][l]
    splits = np.cumsum([3 * D_MODEL, MIX_WIDTH, MIX_WIDTH, MIX_WIDTH, MIX_WIDTH, SSM_CONV_DIM, N_HEADS])
    w_gates, w_q, w_k, w_v, w_z, w_xbc, w_dt, w_rw = jnp.split(w_in, splits.tolist(), axis=1)
    pad = jnp.zeros((D_MODEL, COL_GATES - COL_DT - N_HEADS), F32)
    w_main = jnp.concatenate([w_rw, w_dt, pad, w_gates, w_xbc, w_z, w_v], axis=1).astype(BF16)
    w_qk = jnp.concatenate([w_q, w_k], axis=1)
    row = lambda a: a.reshape(1, -1)
    zeros_lora = jnp.zeros((DECAY_LORA, MIX_WIDTH), F32)
    head_of = np.arange(MIX_WIDTH) // HEAD_DIM
    return dict(
        norm1_g=row(P["norm1_g"][l]), w_main=w_main, w_qk=w_qk.astype(BF16), w_qk_rot=_rot_half_cols(w_qk).astype(BF16),
        ssm_conv_w=P["ssm_conv_w"][l], ssm_conv_b=row(P["ssm_conv_b"][l]),
        dtb_c=P["ssm_dt_bias"][l].reshape(-1, 1), dtb_r=row(P["ssm_dt_bias"][l]),
        alog_c=P["ssm_A_log"][l].reshape(-1, 1), alog_r=row(P["ssm_A_log"][l]),
        dskip=row(jnp.repeat(P["ssm_D"][l], HEAD_DIM)), ssm_norm_g=row(P["ssm_norm_g"][l]),
        rwkv_mu=row(P["rwkv_mu"][l]), rwkv_w0=row(P["rwkv_w0"][l]),
        w2_pad=jnp.concatenate([P["rwkv_w2"][l], zeros_lora], axis=0).astype(BF16),
        rwkv_a0=row(P["rwkv_a0"][l]),
        a2_pad=jnp.concatenate([zeros_lora, P["rwkv_a2"][l]], axis=0).astype(BF16),
        rwkv_g2=P["rwkv_g2"][l].astype(BF16),
        rwkv_k_k=row(P["rwkv_k_k"][l]), rwkv_k_a=row(P["rwkv_k_a"][l]), rwkv_r_k=row(P["rwkv_r_k"][l]),
        rwkv_ln_g=row(P["rwkv_ln_g"][l]), rwkv_ln_b=row(P["rwkv_ln_b"][l]),
        head_seg=jnp.asarray(head_of[:, None] == head_of[None, :], BF16),
        w_br=jnp.stack([P["w_br_att"][l], P["w_br_ssm"][l], P["w_br_rwkv"][l]]).astype(BF16),
        w_out=P["w_out"][l].astype(BF16), norm2_g=row(P["norm2_g"][l]), w_up=P["w_up"][l].astype(BF16),
        ffn_conv_w=P["ffn_conv_w"][l], ffn_conv_b=row(P["ffn_conv_b"][l]), w_down=P["w_down"][l].astype(BF16),
    )


def _tail_rows(a, n_seq, seq, k):
    return a.reshape(n_seq, seq, -1)[:, seq - k:, :]


def _pad_history(h):
    n, k, c = h.shape
    return jnp.concatenate([jnp.zeros((n, SUBLANES - k, c), h.dtype), h], axis=1)


def _pack_pair_lanes(s):
    n, h, r, c = s.shape
    return s.reshape(n, h // 2, 2, r, c).transpose(0, 1, 3, 2, 4).reshape(n, h // 2, r, 2 * c)


def _unpack_pair_lanes(s):
    n, p, r, c2 = s.shape
    return s.reshape(n, p, r, 2, c2 // 2).transpose(0, 1, 3, 2, 4).reshape(n, 2 * p, r, c2 // 2)


def _pack_pair_diag(s):
    n, h, r, c = s.shape
    s = s.reshape(n, h // 2, 2, r, c)
    z = jnp.zeros_like(s[:, :, 0])
    top = jnp.concatenate([s[:, :, 0], z], axis=-1)
    bot = jnp.concatenate([z, s[:, :, 1]], axis=-1)
    return jnp.concatenate([top, bot], axis=-2)


def _unpack_pair_diag(s):
    n, p, r2, c2 = s.shape
    r, c = r2 // 2, c2 // 2
    return jnp.stack([s[:, :, :r, :c], s[:, :, r:, c:]], axis=2).reshape(n, 2 * p, r, c)


def _trunk_layer(x, n_seq, seq, prm, rope_tab, states, attention):
    m = x.shape[0]
    tm = min(512, m)
    fresh = states is None
    main = _norm_matmul(x, prm["norm1_g"], prm["w_main"], tm, MAIN_WIDTH // 4)
    qk = _rope_proj(x, prm["norm1_g"], prm["w_qk"], prm["w_qk_rot"], rope_tab[0], rope_tab[1], min(256, m))
    k_new = qk[:, MIX_WIDTH:]
    v_new = main[:, COL_V:COL_V + MIX_WIDTH]
    o_att = attention(qk, k_new, v_new)

    xbc_tail = _tail_rows(main[:, COL_XBC:COL_XBC + SSM_CONV_DIM], n_seq, seq, SSM_CONV - 1)
    dt_t = main[:, COL_DT:COL_DT + N_HEADS].reshape(n_seq, seq, N_HEADS).transpose(0, 2, 1)
    if fresh:
        conv0 = jnp.zeros((n_seq, SUBLANES, SSM_CONV_DIM), F32)
        ssm0 = jnp.zeros((n_seq, N_PAIRS, SSM_STATE, LANES), F32)
        shift0 = jnp.zeros((n_seq, SUBLANES, RWKV_SHIFT_DIM), F32)
        rwkv0 = jnp.zeros((n_seq, N_PAIRS, LANES, LANES), F32)
    else:
        conv0 = _pad_history(states["ssm_conv"])
        ssm0 = _pack_pair_lanes(states["ssm"])
        shift0 = _pad_history(states["rwkv_shift"][:, None, :])
        rwkv0 = _pack_pair_diag(states["rwkv"])
    o_ssm, ssm_fin = _ssd(main, dt_t, conv0, ssm0, prm, n_seq, seq, math.gcd(seq, SSM_CHUNK))
    o_rwkv, rwkv_fin = _rwkv(main, shift0, rwkv0, prm, n_seq, seq, min(seq, RWKV_CHUNK))
    x = _merge(o_att, o_ssm, o_rwkv, main, x, prm["w_br"], prm["w_out"], tm)

    if fresh:
        x, g_pre = _ffn(x, prm["norm2_g"], prm["w_up"], prm["ffn_conv_w"], prm["ffn_conv_b"], prm["w_down"],
                        n_seq, seq, min(256, seq))
    else:
        hist = states["ffn_conv"]
        p1 = jnp.repeat(hist[:, 1], seq, axis=0)
        p2 = jnp.repeat(hist[:, 0], seq, axis=0)
        x, g_pre = _ffn(x, prm["norm2_g"], prm["w_up"], prm["ffn_conv_w"], prm["ffn_conv_b"], prm["w_down"],
                        n_seq, seq, m, prev=(p1, p2))
    new_states = (
        k_new.reshape(n_seq, seq, N_HEADS, HEAD_DIM), v_new.reshape(n_seq, seq, N_HEADS, HEAD_DIM),
        _unpack_pair_lanes(ssm_fin), xbc_tail, _unpack_pair_diag(rwkv_fin),
        main[:, COL_RW:COL_RW + RWKV_SHIFT_DIM].reshape(n_seq, seq, -1)[:, -1],
        _tail_rows(g_pre, n_seq, seq, FFN_CONV - 1),
    )
    return x, new_states


def _prompt_attention(n_seq, seq):
    nb = seq // MOBA_BLOCK

    def attend(qk, k_new, v_new):
        kmean = _kmean(qk, n_seq, seq)
        bias = _moba_select(qk, kmean, n_seq, seq, MOBA_BLOCK)
        k_blocks = k_new.astype(BF16).reshape(n_seq, nb, MOBA_BLOCK, MIX_WIDTH)
        vt = v_new.astype(BF16).reshape(n_seq, nb, MOBA_BLOCK, N_HEADS, HEAD_DIM).transpose(0, 1, 3, 4, 2)
        extra = jnp.zeros((n_seq, nb, N_HEADS, VT_ROWS - HEAD_DIM, MOBA_BLOCK), BF16).at[:, :, :, 0, :].set(1.0)
        vt_blocks = jnp.concatenate([vt, extra], axis=3).reshape(n_seq, nb, N_HEADS * VT_ROWS, MOBA_BLOCK)
        o_t = _moba_prompt(qk, k_blocks, vt_blocks, bias, n_seq, seq)
        return o_t.transpose(0, 2, 1).reshape(n_seq * seq, MIX_WIDTH)

    return attend


def _transposed_pages(cache):
    l, n, t, h, d = cache.shape
    return cache.transpose(0, 1, 3, 4, 2).reshape(l, n, h * d, t)


def _sample_attention(n_seq, seq, page_table, cache_kt, cache_vt, layer):
    head_of = np.arange(MIX_WIDTH) // HEAD_DIM
    blockdiag = jnp.asarray(np.arange(N_HEADS)[:, None] == head_of[None, :], F32)

    def attend(qk, k_new, v_new):
        q = qk[:, :MIX_WIDTH].reshape(n_seq, 1, seq, MIX_WIDTH) * (HEAD_DIM ** -0.5)
        qbd = (q * blockdiag[None, :, None, :]).reshape(n_seq, N_HEADS * seq, MIX_WIDTH)
        o = _moba_sample(page_table, qbd, cache_kt, cache_vt, layer,
                         k_new.reshape(n_seq, seq, MIX_WIDTH), v_new.reshape(n_seq, seq, MIX_WIDTH))
        return o.reshape(n_seq * seq, MIX_WIDTH)

    return attend


def kernel(x_prompt, x_sample, cache_k, cache_v, page_table, state_ssm, state_ssm_conv, state_rwkv, state_rwkv_shift, state_ffn_conv, norm1_g, w_in, ssm_conv_w, ssm_conv_b, ssm_dt_bias, ssm_A_log, ssm_D, ssm_norm_g, rwkv_mu, rwkv_w0, rwkv_w2, rwkv_a0, rwkv_a2, rwkv_g2, rwkv_k_k, rwkv_k_a, rwkv_r_k, rwkv_ln_g, rwkv_ln_b, w_br_att, w_br_ssm, w_br_rwkv, w_out, norm2_g, w_up, ffn_conv_w, ffn_conv_b, w_down, final_norm_g):
    P = dict(norm1_g=norm1_g, w_in=w_in, ssm_conv_w=ssm_conv_w, ssm_conv_b=ssm_conv_b, ssm_dt_bias=ssm_dt_bias,
             ssm_A_log=ssm_A_log, ssm_D=ssm_D, ssm_norm_g=ssm_norm_g, rwkv_mu=rwkv_mu, rwkv_w0=rwkv_w0,
             rwkv_w2=rwkv_w2, rwkv_a0=rwkv_a0, rwkv_a2=rwkv_a2, rwkv_g2=rwkv_g2, rwkv_k_k=rwkv_k_k,
             rwkv_k_a=rwkv_k_a, rwkv_r_k=rwkv_r_k, rwkv_ln_g=rwkv_ln_g, rwkv_ln_b=rwkv_ln_b, w_br_att=w_br_att,
             w_br_ssm=w_br_ssm, w_br_rwkv=w_br_rwkv, w_out=w_out, norm2_g=norm2_g, w_up=w_up,
             ffn_conv_w=ffn_conv_w, ffn_conv_b=ffn_conv_b, w_down=w_down)
    b_p, t_p, d = x_prompt.shape
    b_s, t_s, _ = x_sample.shape
    n_pages = page_table.shape[1]
    past_len = n_pages * cache_k.shape[2]
    depth = w_in.shape[0]
    xp = x_prompt.reshape(b_p * t_p, d)
    xs = x_sample.reshape(b_s * t_s, d)
    rope_p = _rope_tables(jnp.arange(t_p))
    rope_s = _rope_tables(jnp.tile(past_len + jnp.arange(t_s), b_s))
    att_p = _prompt_attention(b_p, t_p)
    cache_kt = _transposed_pages(cache_k)
    cache_vt = _transposed_pages(cache_v)
    new_p, new_s = [], []
    for l in range(depth):
        prm = _layer_params(l, P)
        xp, st_p = _trunk_layer(xp, b_p, t_p, prm, rope_p, None, att_p)
        states = dict(ssm=state_ssm[l], ssm_conv=state_ssm_conv[l], rwkv=state_rwkv[l],
                      rwkv_shift=state_rwkv_shift[l], ffn_conv=state_ffn_conv[l])
        att_s = _sample_attention(b_s, t_s, page_table, cache_kt, cache_vt, l)
        xs, st_s = _trunk_layer(xs, b_s, t_s, prm, rope_s, states, att_s)
        new_p.append(st_p)
        new_s.append(st_s)
    g_fin = final_norm_g.reshape(1, d)
    y_prompt = _final_norm(xp, g_fin, min(512, xp.shape[0])).reshape(b_p, t_p, d)
    y_sample = _final_norm(xs, g_fin, min(512, xs.shape[0])).reshape(b_s, t_s, d)
    stacked = lambda states, i: jnp.stack([s[i] for s in states])
    outs = [y_prompt, y_sample, stacked(new_p, 0), stacked(new_p, 1), stacked(new_s, 0), stacked(new_s, 1)]
    for i in range(2, 7):
        outs.append(stacked(new_p, i))
        outs.append(stacked(new_s, i))
    return tuple(outs)
```

```python
import functools
import math

import jax
import jax.numpy as jnp
import numpy as np
from jax import lax
from jax.experimental import pallas as pl
from jax.experimental.pallas import tpu as pltpu

F32 = jnp.float32
BF16 = jnp.bfloat16

D_MODEL = 1024
DEPTH = 4
PAGE_SIZE = 128
MIX_WIDTH = D_MODEL // 2
HEAD_DIM = 64
N_HEADS = MIX_WIDTH // HEAD_DIM
N_PAIRS = N_HEADS // 2
MOBA_BLOCK = 256
MOBA_TOPK = 3
KEY_SUB = 128
SAMPLE_BLOCKS_PER_STEP = 4
VT_ROWS = HEAD_DIM + 16
ROPE_THETA = 10000.0
SSM_GROUPS = 2
SSM_STATE = 128
SSM_CONV = 4
SSM_CHUNK = 128
SSM_CONV_DIM = MIX_WIDTH + 2 * SSM_GROUPS * SSM_STATE
DECAY_LORA = 64
ICLR_LORA = 64
GATE_LORA = 128
RWKV_SHIFT_DIM = 3 * MIX_WIDTH + DECAY_LORA + ICLR_LORA + GATE_LORA
RWKV_GN_EPS = 64e-5
RWKV_CHUNK = 128
D_FF = ((8 * D_MODEL // 3 + 127) // 128) * 128
FFN_CONV = 3
EPS = 1e-6
LANES = 128
SUBLANES = 8
NEG = -1e30
VMEM_LIMIT = 52 * 1024 * 1024

COL_RW = 0
COL_DT = RWKV_SHIFT_DIM
COL_GATES = 2048
COL_XBC = COL_GATES + 3 * D_MODEL
COL_Z = COL_XBC + SSM_CONV_DIM
COL_V = COL_Z + MIX_WIDTH
MAIN_WIDTH = COL_V + MIX_WIDTH


def _cparams(*sem):
    return pltpu.CompilerParams(dimension_semantics=sem, vmem_limit_bytes=VMEM_LIMIT)


def _rms(x, g):
    return x * lax.rsqrt(jnp.mean(x * x, axis=-1, keepdims=True) + EPS) * g


def _dot(a, b):
    return jnp.dot(a.astype(BF16), b.astype(BF16), preferred_element_type=F32)


def _dot_nt(a, b):
    return lax.dot_general(a.astype(BF16), b.astype(BF16), (((1,), (1,)), ((), ())), preferred_element_type=F32)


def _dot_tn(a, b):
    return lax.dot_general(a.astype(BF16), b.astype(BF16), (((0,), (0,)), ((), ())), preferred_element_type=F32)


def _dot_f32(a, b):
    return jnp.dot(a, b, preferred_element_type=F32, precision=lax.Precision.HIGHEST)


def _dot_split3(a, b01):
    hi = a.astype(BF16)
    r1 = a - hi.astype(F32)
    mid = r1.astype(BF16)
    lo = (r1 - mid.astype(F32)).astype(BF16)
    b = b01.astype(BF16)
    dot = lambda x: jnp.dot(x, b, preferred_element_type=F32)
    return dot(hi) + dot(mid) + dot(lo)


def _silu(x):
    return x * jax.nn.sigmoid(x)


def _softplus(x):
    return jnp.maximum(x, 0.0) + jnp.log(1.0 + jnp.exp(-jnp.abs(x)))


def _shift_rows(g, k, prevs, period):
    rows = g.shape[0]
    out = pltpu.roll(g, k, axis=0)
    tig = lax.broadcasted_iota(jnp.int32, (rows, 1), 0) & (period - 1)
    for t0 in range(k):
        out = jnp.where(tig == t0, prevs[k - t0 - 1], out)
    return out


def _norm_matmul_kernel(x_ref, g_ref, w_ref, o_ref):
    o_ref[...] = _dot(_rms(x_ref[...], g_ref[...]), w_ref[...])


def _norm_matmul(x, g, w, tm, tn):
    m, d = x.shape
    n = w.shape[1]
    return pl.pallas_call(
        _norm_matmul_kernel,
        grid=(n // tn, m // tm),
        in_specs=[pl.BlockSpec((tm, d), lambda j, i: (i, 0)),
                  pl.BlockSpec((1, d), lambda j, i: (0, 0)),
                  pl.BlockSpec((d, tn), lambda j, i: (0, j))],
        out_specs=pl.BlockSpec((tm, tn), lambda j, i: (i, j)),
        out_shape=jax.ShapeDtypeStruct((m, n), F32),
        compiler_params=_cparams("arbitrary", "arbitrary"),
        name="norm_matmul",
    )(x, g, w)


def _rope_proj_kernel(x_ref, g_ref, w_ref, wr_ref, cos_ref, sin_ref, q_ref, k_ref, kb_ref):
    h = _rms(x_ref[...], g_ref[...]).astype(BF16)
    a = jnp.dot(h, w_ref[...], preferred_element_type=F32)
    b = jnp.dot(h, wr_ref[...], preferred_element_type=F32)
    reps = a.shape[1] // LANES
    cos = jnp.concatenate([cos_ref[...]] * reps, axis=1)
    sin = jnp.concatenate([sin_ref[...]] * reps, axis=1)
    qk = a * cos + b * sin
    half = qk.shape[1] // 2
    q_ref[...] = qk[:, :half]
    k_ref[...] = qk[:, half:]
    kb_ref[...] = qk[:, half:].astype(BF16)


def _rope_proj(x, g, w, wr, cos, sin, tm):
    m, d = x.shape
    n = w.shape[1]
    half = n // 2
    tab_blocks = cos.shape[0] // tm
    return pl.pallas_call(
        _rope_proj_kernel,
        grid=(m // tm,),
        in_specs=[pl.BlockSpec((tm, d), lambda i: (i, 0)),
                  pl.BlockSpec((1, d), lambda i: (0, 0)),
                  pl.BlockSpec((d, n), lambda i: (0, 0)),
                  pl.BlockSpec((d, n), lambda i: (0, 0)),
                  pl.BlockSpec((tm, LANES), lambda i: (i % tab_blocks, 0)),
                  pl.BlockSpec((tm, LANES), lambda i: (i % tab_blocks, 0))],
        out_specs=[pl.BlockSpec((tm, half), lambda i: (i, 0))] * 3,
        out_shape=[jax.ShapeDtypeStruct((m, half), F32), jax.ShapeDtypeStruct((m, half), F32),
                   jax.ShapeDtypeStruct((m, half), BF16)],
        compiler_params=_cparams("arbitrary"),
        name="rope_proj",
    )(x, g, w, wr, cos, sin)


def _kmean_kernel(k_ref, o_ref):
    k = k_ref[...]
    o_ref[...] = jnp.mean(k.reshape(SUBLANES, MOBA_BLOCK, k.shape[-1]), axis=1)


def _kmean(k2d, n_seq, seq):
    nb = seq // MOBA_BLOCK
    w = k2d.shape[1]
    rows = SUBLANES * MOBA_BLOCK
    return pl.pallas_call(
        _kmean_kernel,
        grid=(n_seq * nb // SUBLANES,),
        in_specs=[pl.BlockSpec((rows, w), lambda i: (i, 0))],
        out_specs=pl.BlockSpec((SUBLANES, w), lambda i: (i, 0)),
        out_shape=jax.ShapeDtypeStruct((n_seq * nb, w), F32),
        compiler_params=_cparams("arbitrary"),
        name="moba_kmean",
    )(k2d).reshape(n_seq, nb, w)


def _head_mask(pair_vals, hl):
    lane = lax.broadcasted_iota(jnp.int32, pair_vals.shape, 1)
    return jnp.where((lane >= hl * HEAD_DIM) & (lane < (hl + 1) * HEAD_DIM), pair_vals, jnp.zeros_like(pair_vals))


def _moba_select_kernel(q_ref, km_ref, o_ref, *, nb):
    tq = q_ref.shape[0]
    q0 = pl.program_id(1) * tq
    qpos = q0 + lax.broadcasted_iota(jnp.int32, (1, tq), 1)
    own = qpos // MOBA_BLOCK
    kb = lax.broadcasted_iota(jnp.int32, (nb, tq), 0)
    kbf = kb.astype(F32)
    valid = kb < own
    for h in range(N_HEADS):
        p, hl = divmod(h, 2)
        lanes = slice(p * LANES, (p + 1) * LANES)
        km = _head_mask(km_ref[:, lanes], hl)
        gate = lax.dot_general(km, q_ref[:, lanes], (((1,), (1,)), ((), ())),
                               preferred_element_type=F32, precision=lax.Precision.HIGHEST)
        g = jnp.where(valid, gate, -jnp.inf)
        sel = jnp.zeros((nb, tq), jnp.bool_)
        for _ in range(MOBA_TOPK):
            m = jnp.max(g, axis=0, keepdims=True)
            first = jnp.min(jnp.where(g == m, kbf, float(nb)), axis=0, keepdims=True)
            hit = (kbf == first) & (m > -jnp.inf)
            sel = sel | hit
            g = jnp.where(hit, -jnp.inf, g)
        o_ref[h] = jnp.where(sel, 0.0, NEG)


def _moba_select(q2d, kmean, n_seq, seq, tq):
    nb = seq // MOBA_BLOCK
    w = q2d.shape[1]
    tiles = seq // tq
    return pl.pallas_call(
        functools.partial(_moba_select_kernel, nb=nb),
        grid=(n_seq, tiles),
        in_specs=[pl.BlockSpec((tq, w), lambda b, i: (b * tiles + i, 0)),
                  pl.BlockSpec((None, nb, w), lambda b, i: (b, 0, 0))],
        out_specs=pl.BlockSpec((None, N_HEADS, nb, tq), lambda b, i: (b, 0, 0, i)),
        out_shape=jax.ShapeDtypeStruct((n_seq, N_HEADS, nb, seq), F32),
        compiler_params=_cparams("arbitrary", "arbitrary"),
        name="moba_select",
    )(q2d, kmean)


def _moba_prompt_kernel(q_ref, k_ref, vt_ref, bias_ref, o_ref, qm_scr, m_scr, acc_scr, s_scr):
    tq = q_ref.shape[0]
    j = pl.program_id(1)
    scale = HEAD_DIM ** -0.5 * math.log2(math.e)
    key_i = lax.broadcasted_iota(jnp.int32, (MOBA_BLOCK, tq), 0)
    qry_i = lax.broadcasted_iota(jnp.int32, (MOBA_BLOCK, tq), 1)
    causal = key_i <= qry_i
    for h in range(N_HEADS):
        lanes = slice((h // 2) * LANES, (h // 2 + 1) * LANES)
        rows = slice(h * VT_ROWS, (h + 1) * VT_ROWS)
        qm = _head_mask((q_ref[:, lanes] * scale).astype(BF16), h % 2)
        qm_scr[h] = qm
        s = jnp.where(causal, _dot_nt(k_ref[j, :, lanes], qm), NEG)
        m = jnp.max(s, axis=0, keepdims=True)
        m_scr[h:h + 1, :] = m
        acc_scr[rows, :] = _dot(vt_ref[j, rows, :], jnp.exp2(s - m))
        s_scr[0, h] = _dot_nt(k_ref[0, 0:KEY_SUB, lanes], qm)

    def body(kb, carry):
        for sub in range(MOBA_BLOCK // KEY_SUB):
            cur = sub % 2
            if sub + 1 < MOBA_BLOCK // KEY_SUB:
                kn, r0 = kb, (sub + 1) * KEY_SUB
            else:
                kn, r0 = jnp.minimum(kb + 1, j), 0
            for h in range(N_HEADS):
                lanes = slice((h // 2) * LANES, (h // 2 + 1) * LANES)
                s_scr[1 - cur, h] = _dot_nt(k_ref[kn, r0:r0 + KEY_SUB, lanes], qm_scr[h])
            for h in range(N_HEADS):
                rows = slice(h * VT_ROWS, (h + 1) * VT_ROWS)
                s = s_scr[cur, h] + bias_ref[h, pl.ds(kb, 1), :]
                m = m_scr[h:h + 1, :]
                m_new = jnp.maximum(m, jnp.max(s, axis=0, keepdims=True))
                m_scr[h:h + 1, :] = m_new
                acc_scr[rows, :] = (acc_scr[rows, :] * jnp.exp2(m - m_new)
                                    + _dot(vt_ref[kb, rows, sub * KEY_SUB:(sub + 1) * KEY_SUB], jnp.exp2(s - m_new)))
        return carry

    lax.fori_loop(0, j, body, 0)
    for h in range(N_HEADS):
        r0 = h * VT_ROWS
        o_ref[h * HEAD_DIM:(h + 1) * HEAD_DIM, :] = (acc_scr[r0:r0 + HEAD_DIM, :]
                                                     / acc_scr[r0 + HEAD_DIM:r0 + HEAD_DIM + 1, :])


def _moba_prompt(q2d, k_blocks, vt_blocks, bias, n_seq, seq):
    nb = seq // MOBA_BLOCK
    w = MIX_WIDTH
    tq = MOBA_BLOCK
    return pl.pallas_call(
        _moba_prompt_kernel,
        grid=(n_seq, nb),
        in_specs=[pl.BlockSpec((tq, w), lambda b, i: (b * nb + i, 0)),
                  pl.BlockSpec((None, nb, MOBA_BLOCK, w), lambda b, i: (b, 0, 0, 0)),
                  pl.BlockSpec((None, nb, N_HEADS * VT_ROWS, MOBA_BLOCK), lambda b, i: (b, 0, 0, 0)),
                  pl.BlockSpec((None, N_HEADS, nb, tq), lambda b, i: (b, 0, 0, i))],
        out_specs=pl.BlockSpec((None, w, tq), lambda b, i: (b, 0, i)),
        out_shape=jax.ShapeDtypeStruct((n_seq, w, seq), F32),
        scratch_shapes=[pltpu.VMEM((N_HEADS, tq, LANES), BF16), pltpu.VMEM((N_HEADS, tq), F32),
                        pltpu.VMEM((N_HEADS * VT_ROWS, tq), F32),
                        pltpu.VMEM((2, N_HEADS, KEY_SUB, tq), F32)],
        compiler_params=_cparams("arbitrary", "arbitrary"),
        name="moba_prompt",
    )(q2d, k_blocks, vt_blocks, bias)


def _moba_sample_kernel(pt_ref, qbd_ref, *refs, nb, blocks_per_step):
    del pt_ref
    n_pages = 2 * blocks_per_step
    k_refs = refs[:n_pages]
    v_refs = refs[n_pages:2 * n_pages]
    kn_ref, vn_ref, o_ref, r_scr, m_scr, l_scr, g_scr = refs[2 * n_pages:]
    step = pl.program_id(1)
    nq = qbd_ref.shape[0]
    lane = lax.broadcasted_iota(jnp.int32, (nq, LANES), 1)

    @pl.when(step == 0)
    def _():
        m_scr[...] = jnp.full(m_scr.shape, NEG, F32)
        l_scr[...] = jnp.zeros(l_scr.shape, F32)
        g_scr[...] = jnp.full(g_scr.shape, -jnp.inf, F32)

    qbd = qbd_ref[...].astype(BF16)
    m_all, l_all, g_all = m_scr[...], l_scr[...], g_scr[...]
    blocks = range(blocks_per_step)
    s = [_dot(qbd, jnp.concatenate([k_refs[2 * i][...], k_refs[2 * i + 1][...]], axis=1)) for i in blocks]
    m_b = [jnp.max(s[i], axis=-1, keepdims=True) for i in blocks]
    p = [jnp.exp(s[i] - m_b[i]) for i in blocks]
    for i in blocks:
        b = step * blocks_per_step + i
        vt = jnp.concatenate([v_refs[2 * i][...], v_refs[2 * i + 1][...]], axis=1)
        r_scr[b] = _dot_nt(p[i], vt)
        col = lane == b
        m_all = jnp.where(col, m_b[i], m_all)
        l_all = jnp.where(col, jnp.sum(p[i], axis=-1, keepdims=True), l_all)
        g_all = jnp.where(col, jnp.sum(s[i], axis=-1, keepdims=True), g_all)
    m_scr[...] = m_all
    l_scr[...] = l_all
    g_scr[...] = g_all

    @pl.when(step == nb // blocks_per_step - 1)
    def _():
        lane_f = lane.astype(F32)
        g = g_scr[...]
        sel = jnp.zeros((nq, LANES), jnp.bool_)
        for _ in range(MOBA_TOPK):
            mx = jnp.max(g, axis=-1, keepdims=True)
            first = jnp.min(jnp.where(g == mx, lane_f, float(LANES)), axis=-1, keepdims=True)
            hit = (lane_f == first) & (mx > -jnp.inf)
            sel = sel | hit
            g = jnp.where(hit, -jnp.inf, g)
        n_new = kn_ref.shape[0]
        s_own = _dot_nt(qbd, kn_ref[...])
        q_t = lax.broadcasted_iota(jnp.int32, (nq, n_new), 0) & (n_new - 1)
        k_t = lax.broadcasted_iota(jnp.int32, (nq, n_new), 1)
        s_own = jnp.where(k_t <= q_t, s_own, NEG)
        m_all = m_scr[...]
        m_tot = jnp.maximum(jnp.max(jnp.where(sel, m_all, NEG), axis=-1, keepdims=True),
                            jnp.max(s_own, axis=-1, keepdims=True))
        w_sel = jnp.where(sel, jnp.exp(m_all - m_tot), 0.0)
        p_own = jnp.exp(s_own - m_tot)
        l_tot = jnp.sum(w_sel * l_scr[...], axis=-1, keepdims=True) + jnp.sum(p_own, axis=-1, keepdims=True)
        acc = _dot(p_own, vn_ref[...])
        for bb in range(nb):
            acc = acc + w_sel[:, bb:bb + 1] * r_scr[bb]
        acc = acc / l_tot
        lane_w = lax.broadcasted_iota(jnp.int32, (n_new, acc.shape[1]), 1) // HEAD_DIM
        out = jnp.zeros((n_new, acc.shape[1]), F32)
        for h in range(N_HEADS):
            out = out + jnp.where(lane_w == h, acc[h * n_new:(h + 1) * n_new, :], 0.0)
        o_ref[...] = out


def _moba_sample(page_table, qbd, cache_kt, cache_vt, layer, k_new, v_new):
    n_seq, nq, w = qbd.shape
    t_new = k_new.shape[1]
    n_pages = page_table.shape[1]
    nb = n_pages * PAGE_SIZE // MOBA_BLOCK
    bps = SAMPLE_BLOCKS_PER_STEP
    assert MOBA_BLOCK == 2 * PAGE_SIZE and nb <= LANES and nb % bps == 0
    page = lambda off: pl.BlockSpec((None, None, w, PAGE_SIZE),
                                    lambda s, b, pt: (layer, pt[s * n_pages + 2 * bps * b + off], 0, 0))
    pages = [page(off) for off in range(2 * bps)]
    grid_spec = pltpu.PrefetchScalarGridSpec(
        num_scalar_prefetch=1,
        grid=(n_seq, nb // bps),
        in_specs=[pl.BlockSpec((None, nq, w), lambda s, b, pt: (s, 0, 0))] + pages + pages
                 + [pl.BlockSpec((None, t_new, w), lambda s, b, pt: (s, 0, 0)),
                    pl.BlockSpec((None, t_new, w), lambda s, b, pt: (s, 0, 0))],
        out_specs=pl.BlockSpec((None, t_new, w), lambda s, b, pt: (s, 0, 0)),
        scratch_shapes=[pltpu.VMEM((nb, nq, w), F32), pltpu.VMEM((nq, LANES), F32),
                        pltpu.VMEM((nq, LANES), F32), pltpu.VMEM((nq, LANES), F32)],
    )
    return pl.pallas_call(
        functools.partial(_moba_sample_kernel, nb=nb, blocks_per_step=bps),
        grid_spec=grid_spec,
        out_shape=jax.ShapeDtypeStruct((n_seq, t_new, w), F32),
        compiler_params=_cparams("arbitrary", "arbitrary"),
        name="moba_sample",
    )(page_table.reshape(-1), qbd, *([cache_kt] * (2 * bps)), *([cache_vt] * (2 * bps)), k_new, v_new)


def _ssd_kernel(xbc_ref, z_ref, dtc_ref, dtt_ref, conv0_ref, s0_ref, cw_ref, cb_ref, dtb_c_ref, dtb_r_ref,
                a_c_ref, a_r_ref, dskip_ref, ng_ref, y_ref, sfin_ref, carry_scr, s_scr):
    c = pl.program_id(1)
    q = xbc_ref.shape[0]

    @pl.when(c == 0)
    def _():
        carry_scr[...] = conv0_ref[...]
        s_scr[...] = s0_ref[...]

    xbc = xbc_ref[...]
    carry = carry_scr[...]
    prevs = [carry[SUBLANES - i:SUBLANES - i + 1, :] for i in range(1, SSM_CONV)]
    conv = cb_ref[...] + cw_ref[SSM_CONV - 1:SSM_CONV, :] * xbc
    for k in range(1, SSM_CONV):
        conv = conv + cw_ref[SSM_CONV - 1 - k:SSM_CONV - k, :] * _shift_rows(xbc, k, prevs, q)
    carry_scr[...] = xbc[q - SUBLANES:, :]
    conv = _silu(conv)
    xs = conv[:, :MIX_WIDTH]
    gw = SSM_STATE
    bmat = [conv[:, MIX_WIDTH + g * gw:MIX_WIDTH + (g + 1) * gw] for g in range(SSM_GROUPS)]
    cmat = [conv[:, MIX_WIDTH + (SSM_GROUPS + g) * gw:MIX_WIDTH + (SSM_GROUPS + g + 1) * gw] for g in range(SSM_GROUPS)]

    dt_c = _softplus(dtc_ref[:, :N_HEADS] + dtb_r_ref[...])
    dt_t = _softplus(dtt_ref[...] + dtb_c_ref[...])
    a_r = -jnp.exp(a_r_ref[...])
    a_c = -jnp.exp(a_c_ref[...])
    ri = lax.broadcasted_iota(jnp.int32, (q, q), 0)
    ci = lax.broadcasted_iota(jnp.int32, (q, q), 1)
    lower = (ri >= ci).astype(F32)
    cum_c = _dot_f32(lower, dt_c * a_r)
    cum_t = _dot_f32(dt_t * a_c, (ri <= ci).astype(F32))
    causal = ri >= ci
    cb = [_dot_nt(cmat[g], bmat[g]) for g in range(SSM_GROUPS)]
    heads_per_group = N_HEADS // SSM_GROUPS
    lane = lax.broadcasted_iota(jnp.int32, (1, LANES), 1)
    first_half = lane < HEAD_DIM

    for p in range(N_PAIRS):
        lanes = slice(p * LANES, (p + 1) * LANES)
        xs_p = xs[:, lanes]
        s_in = s_scr[p]
        y_h, sc_h, cd_h = [], [], []
        for hl in range(2):
            h = 2 * p + hl
            g = h // heads_per_group
            cc = cum_c[:, h:h + 1]
            ct = cum_t[h:h + 1, :]
            lmat = jnp.where(causal, jnp.exp(jnp.where(causal, cc - ct, 0.0)), 0.0)
            y_diag = _dot(cb[g] * lmat * dt_t[h:h + 1, :], xs_p)
            y_off = _dot(cmat[g] * jnp.exp(cc), s_in)
            y_h.append(y_diag + y_off)
            dec_end = jnp.exp(cc[q - 1:q, :] - cc)
            sc_h.append(_dot_tn(bmat[g] * (dec_end * dt_c[:, h:h + 1]), xs_p))
            cd_h.append(jnp.exp(cc[q - 1:q, :]))
        y_ref[:, lanes] = jnp.where(first_half, y_h[0], y_h[1])
        s_scr[p] = (s_in * jnp.where(first_half, cd_h[0], cd_h[1])
                    + jnp.where(first_half, sc_h[0], sc_h[1]))

    y = (y_ref[...] + dskip_ref[...] * xs) * _silu(z_ref[...])
    gwid = MIX_WIDTH // SSM_GROUPS
    for g in range(SSM_GROUPS):
        cols = slice(g * gwid, (g + 1) * gwid)
        y_ref[:, cols] = _rms(y[:, cols], ng_ref[:, cols])
    sfin_ref[...] = s_scr[...]


def _ssd(main, dt_t, conv0, s0, prm, n_seq, seq, chunk):
    nc = seq // chunk
    xw = SSM_CONV_DIM
    row = lambda b, c: b * nc + c
    const2 = lambda b, c: (0, 0)
    return pl.pallas_call(
        _ssd_kernel,
        grid=(n_seq, nc),
        in_specs=[pl.BlockSpec((chunk, xw), lambda b, c: (row(b, c), COL_XBC // xw)),
                  pl.BlockSpec((chunk, MIX_WIDTH), lambda b, c: (row(b, c), COL_Z // MIX_WIDTH)),
                  pl.BlockSpec((chunk, LANES), lambda b, c: (row(b, c), COL_DT // LANES)),
                  pl.BlockSpec((None, N_HEADS, chunk), lambda b, c: (b, 0, c)),
                  pl.BlockSpec((None, SUBLANES, xw), lambda b, c: (b, 0, 0)),
                  pl.BlockSpec((None, N_PAIRS, SSM_STATE, LANES), lambda b, c: (b, 0, 0, 0)),
                  pl.BlockSpec((SSM_CONV, xw), const2),
                  pl.BlockSpec((1, xw), const2),
                  pl.BlockSpec((N_HEADS, 1), const2),
                  pl.BlockSpec((1, N_HEADS), const2),
                  pl.BlockSpec((N_HEADS, 1), const2),
                  pl.BlockSpec((1, N_HEADS), const2),
                  pl.BlockSpec((1, MIX_WIDTH), const2),
                  pl.BlockSpec((1, MIX_WIDTH), const2)],
        out_specs=[pl.BlockSpec((chunk, MIX_WIDTH), lambda b, c: (row(b, c), 0)),
                   pl.BlockSpec((None, N_PAIRS, SSM_STATE, LANES), lambda b, c: (b, 0, 0, 0))],
        out_shape=[jax.ShapeDtypeStruct((n_seq * seq, MIX_WIDTH), F32),
                   jax.ShapeDtypeStruct((n_seq, N_PAIRS, SSM_STATE, LANES), F32)],
        scratch_shapes=[pltpu.VMEM((SUBLANES, xw), F32), pltpu.VMEM((N_PAIRS, SSM_STATE, LANES), F32)],
        compiler_params=_cparams("arbitrary", "arbitrary"),
        name="ssd",
    )(main, main, main, dt_t, conv0, s0, prm["ssm_conv_w"], prm["ssm_conv_b"], prm["dtb_c"], prm["dtb_r"],
      prm["alog_c"], prm["alog_r"], prm["dskip"], prm["ssm_norm_g"])


def _rwkv_kernel(rw_ref, shift0_ref, s0_ref, mu_ref, w0_ref, w2_ref, a0_ref, a2_ref, g2_ref, kk_ref, ka_ref,
                 rk_ref, lng_ref, lnb_ref, seg_ref, y_ref, sfin_ref, carry_scr, s_scr):
    c = pl.program_id(1)
    n_t = rw_ref.shape[0]
    w = MIX_WIDTH

    @pl.when(c == 0)
    def _():
        carry_scr[...] = shift0_ref[...]
        s_scr[...] = s0_ref[...]

    p_in = rw_ref[...]
    p_prev = _shift_rows(p_in, 1, [carry_scr[SUBLANES - 1:SUBLANES, :]], n_t)
    carry_scr[...] = p_in[n_t - SUBLANES:, :]
    xs = p_in + (p_prev - p_in) * mu_ref[...]
    r = xs[:, 0:w]
    k = xs[:, w:2 * w]
    v = xs[:, 2 * w:3 * w]
    lora_in = xs[:, 3 * w:3 * w + LANES]
    gate_in = xs[:, 3 * w + LANES:]
    w_log = -_softplus(-(w0_ref[...] + _dot(jnp.tanh(lora_in), w2_ref[...]))) - 0.5
    logw = -jnp.exp(w_log)
    a = jax.nn.sigmoid(a0_ref[...] + _dot(lora_in, a2_ref[...]))
    g = _dot(jax.nn.sigmoid(gate_in), g2_ref[...])
    seg = seg_ref[...]
    kk = k * kk_ref[...]
    kk = kk * lax.rsqrt(jnp.maximum(_dot_split3(kk * kk, seg), 1e-12))
    k = k * (1.0 + (a - 1.0) * ka_ref[...])
    bonus = _dot_split3(r * k * rk_ref[...], seg) * v

    ri = lax.broadcasted_iota(jnp.int32, (n_t, n_t), 0)
    ci = lax.broadcasted_iota(jnp.int32, (n_t, n_t), 1)
    strict = ri > ci
    incl = ri >= ci
    cl = _dot_f32(incl.astype(F32), logw)
    mid = n_t // 2 - 1
    cl_mid = cl[mid:mid + 1, :]
    cm = cl - cl_mid
    p_inv = jnp.exp(-cm)
    a_t = -kk * jnp.exp(cm - logw)
    b_t = kk * a * p_inv
    k_t = k * p_inv
    r_t = r * jnp.exp(cm)
    to_mid = jnp.exp(cl_mid)
    mid_to_end = jnp.exp(cm[n_t - 1:n_t, :])
    p_end = jnp.exp(cl[n_t - 1:n_t, :])
    n_dbl = max(int(math.log2(n_t)) - 1, 0)
    first_half = lax.broadcasted_iota(jnp.int32, (1, LANES), 1) < HEAD_DIM
    heads = range(N_HEADS)
    pair_lanes = [slice((h // 2) * LANES, (h // 2 + 1) * LANES) for h in heads]
    am = [_head_mask(a_t[:, pair_lanes[h]], h % 2) for h in heads]
    rm = [_head_mask(r_t[:, pair_lanes[h]], h % 2) for h in heads]
    bm = [_head_mask(b_t[:, pair_lanes[h]], h % 2) for h in heads]
    km = [_head_mask(k_t[:, pair_lanes[h]], h % 2) for h in heads]
    vp = [v[:, pair_lanes[h]] for h in heads]
    gram = [_dot_nt(jnp.concatenate([am[h], rm[h]], axis=0), jnp.concatenate([bm[h], km[h]], axis=0))
            for h in heads]
    l_pow = [jnp.where(strict, gram[h][:n_t, :n_t], 0.0) for h in heads]
    l_ak = [jnp.where(strict, gram[h][:n_t, n_t:], 0.0) for h in heads]
    m_rb = [jnp.where(incl, gram[h][n_t:, :n_t], 0.0) for h in heads]
    m_rk = [jnp.where(incl, gram[h][n_t:, n_t:], 0.0) for h in heads]
    wu = [jnp.concatenate([am[h], _dot(l_ak[h], vp[h])], axis=1) for h in heads]
    wu = [wu[h] + _dot(l_pow[h], wu[h]) for h in heads]
    for _ in range(n_dbl):
        l_pow = [_dot(l_pow[h], l_pow[h]) for h in heads]
        wu = [wu[h] + _dot(l_pow[h], wu[h]) for h in heads]
    mw = [_dot(m_rb[h], wu[h]) for h in heads]
    mv = [_dot(m_rk[h], vp[h]) for h in heads]
    bw = [_dot_tn(wu[h], bm[h]) for h in heads]
    top = lax.broadcasted_iota(jnp.int32, (LANES, LANES), 0) < HEAD_DIM
    diag = top == (lax.broadcasted_iota(jnp.int32, (LANES, LANES), 1) < HEAD_DIM)
    for p in range(N_PAIRS):
        h0, h1 = 2 * p, 2 * p + 1
        lanes = pair_lanes[h0]
        q_hat = rm[h0] + mw[h0][:, :LANES] + rm[h1] + mw[h1][:, :LANES]
        y_hat = jnp.where(first_half, mw[h0][:, LANES:] + mv[h0], mw[h1][:, LANES:] + mv[h1])
        s_in = s_scr[p]
        s_mid = s_in * to_mid[:, lanes]
        y_ref[:, lanes] = _dot_nt(q_hat, s_mid) + y_hat
        vk = _dot_tn(vp[h0], k_t[:, lanes])
        h_t = jnp.where(diag, jnp.where(top, bw[h0][LANES:, :], bw[h1][LANES:, :]) + vk, 0.0)
        wb = bw[h0][:LANES, :] + bw[h1][:LANES, :]
        s_scr[p] = s_in * p_end[:, lanes] + (_dot(s_mid, wb) + h_t) * mid_to_end[:, lanes]

    y = y_ref[...]
    mean = _dot_split3(y, seg) * (1.0 / HEAD_DIM)
    yc = y - mean
    var = _dot_split3(yc * yc, seg) * (1.0 / HEAD_DIM)
    yn = yc * lax.rsqrt(var + RWKV_GN_EPS) * lng_ref[...] + lnb_ref[...]
    y_ref[...] = (yn + bonus) * g
    sfin_ref[...] = s_scr[...]


def _rwkv(main, shift0, s0, prm, n_seq, seq, chunk):
    nc = seq // chunk
    rw = RWKV_SHIFT_DIM
    w = MIX_WIDTH
    const2 = lambda b, c: (0, 0)
    vec = pl.BlockSpec((1, w), const2)
    return pl.pallas_call(
        _rwkv_kernel,
        grid=(n_seq, nc),
        in_specs=[pl.BlockSpec((chunk, rw), lambda b, c: (b * nc + c, COL_RW // rw)),
                  pl.BlockSpec((None, SUBLANES, rw), lambda b, c: (b, 0, 0)),
                  pl.BlockSpec((None, N_PAIRS, LANES, LANES), lambda b, c: (b, 0, 0, 0)),
                  pl.BlockSpec((1, rw), const2),
                  vec, pl.BlockSpec((LANES, w), const2),
                  vec, pl.BlockSpec((LANES, w), const2),
                  pl.BlockSpec((GATE_LORA, w), const2),
                  vec, vec, vec, vec, vec,
                  pl.BlockSpec((w, w), const2)],
        out_specs=[pl.BlockSpec((chunk, w), lambda b, c: (b * nc + c, 0)),
                   pl.BlockSpec((None, N_PAIRS, LANES, LANES), lambda b, c: (b, 0, 0, 0))],
        out_shape=[jax.ShapeDtypeStruct((n_seq * seq, w), F32),
                   jax.ShapeDtypeStruct((n_seq, N_PAIRS, LANES, LANES), F32)],
        scratch_shapes=[pltpu.VMEM((SUBLANES, rw), F32), pltpu.VMEM((N_PAIRS, LANES, LANES), F32)],
        compiler_params=_cparams("arbitrary", "arbitrary"),
        name="rwkv7",
    )(main, shift0, s0, prm["rwkv_mu"], prm["rwkv_w0"], prm["w2_pad"], prm["rwkv_a0"], prm["a2_pad"],
      prm["rwkv_g2"], prm["rwkv_k_k"], prm["rwkv_k_a"], prm["rwkv_r_k"], prm["rwkv_ln_g"], prm["rwkv_ln_b"],
      prm["head_seg"])


def _merge_kernel(oa_ref, os_ref, or_ref, ga_ref, gs_ref, gr_ref, x_ref, wbr_ref, wo_ref, o_ref):
    merged = (jax.nn.sigmoid(ga_ref[...]) * _dot(oa_ref[...], wbr_ref[0])
              + jax.nn.sigmoid(gs_ref[...]) * _dot(os_ref[...], wbr_ref[1])
              + jax.nn.sigmoid(gr_ref[...]) * _dot(or_ref[...], wbr_ref[2]))
    o_ref[...] = x_ref[...] + _dot(merged, wo_ref[...])


def _merge(o_att, o_ssm, o_rwkv, main, x, w_br, w_out, tm):
    m, d = x.shape
    w = MIX_WIDTH
    act = pl.BlockSpec((tm, w), lambda i: (i, 0))
    gate = lambda b: pl.BlockSpec((tm, d), lambda i: (i, COL_GATES // d + b))
    return pl.pallas_call(
        _merge_kernel,
        grid=(m // tm,),
        in_specs=[act, act, act, gate(0), gate(1), gate(2),
                  pl.BlockSpec((tm, d), lambda i: (i, 0)),
                  pl.BlockSpec((3, w, d), lambda i: (0, 0, 0)),
                  pl.BlockSpec((d, d), lambda i: (0, 0))],
        out_specs=pl.BlockSpec((tm, d), lambda i: (i, 0)),
        out_shape=jax.ShapeDtypeStruct((m, d), F32),
        compiler_params=_cparams("arbitrary"),
        name="merge_out",
    )(o_att, o_ssm, o_rwkv, main, main, main, x, w_br, w_out)


def _ffn_kernel(x_ref, g_ref, wup_ref, cw_ref, cb_ref, wdn_ref, *rest, period, carried):
    if carried:
        o_ref, gp_ref, carry_scr = rest
    else:
        p1_ref, p2_ref, o_ref, gp_ref = rest
    x = x_ref[...]
    up = _dot(_rms(x, g_ref[...]), wup_ref[...])
    g_pre = up[:, :D_FF]
    u = up[:, D_FF:]
    tm = x.shape[0]
    gp_ref[...] = g_pre[tm - gp_ref.shape[0]:, :]
    if carried:
        @pl.when(pl.program_id(1) == 0)
        def _():
            carry_scr[...] = jnp.zeros(carry_scr.shape, F32)
        prevs = [carry_scr[SUBLANES - i:SUBLANES - i + 1, :] for i in range(1, FFN_CONV)]
    else:
        prevs = [p1_ref[...], p2_ref[...]]
    conv = cb_ref[...] + cw_ref[FFN_CONV - 1:FFN_CONV, :] * g_pre
    for k in range(1, FFN_CONV):
        conv = conv + cw_ref[FFN_CONV - 1 - k:FFN_CONV - k, :] * _shift_rows(g_pre, k, prevs, period)
    if carried:
        carry_scr[...] = g_pre[tm - SUBLANES:, :]
    o_ref[...] = x + _dot(_silu(conv) * u, wdn_ref[...])


def _ffn(x, g, w_up, conv_w, conv_b, w_down, n_seq, seq, tm, prev=None):
    m, d = x.shape
    carried = prev is None
    const = lambda *_: (0, 0)
    if carried:
        tiles = seq // tm
        grid = (n_seq, tiles)
        row = lambda b, i: (b * tiles + i, 0)
        period = tm
        extra_in, extra_args = [], []
        scratch = [pltpu.VMEM((SUBLANES, D_FF), F32)]
        gp_rows = SUBLANES
    else:
        assert m == tm
        grid = (1, 1)
        row = lambda b, i: (0, 0)
        period = seq
        extra_in = [pl.BlockSpec((tm, D_FF), row)] * 2
        extra_args = list(prev)
        scratch = []
        gp_rows = tm
    x_new, gp = pl.pallas_call(
        functools.partial(_ffn_kernel, period=period, carried=carried),
        grid=grid,
        in_specs=[pl.BlockSpec((tm, d), row), pl.BlockSpec((1, d), const),
                  pl.BlockSpec((d, 2 * D_FF), const), pl.BlockSpec((FFN_CONV, D_FF), const),
                  pl.BlockSpec((1, D_FF), const), pl.BlockSpec((D_FF, d), const)] + extra_in,
        out_specs=[pl.BlockSpec((tm, d), row), pl.BlockSpec((gp_rows, D_FF), row)],
        out_shape=[jax.ShapeDtypeStruct((m, d), F32), jax.ShapeDtypeStruct((m // tm * gp_rows, D_FF), F32)],
        scratch_shapes=scratch,
        compiler_params=_cparams("arbitrary", "arbitrary"),
        name="conv_ffn",
    )(x, g, w_up, conv_w, conv_b, w_down, *extra_args)
    if carried:
        hist = gp.reshape(n_seq, seq // tm, SUBLANES, D_FF)[:, -1, SUBLANES - (FFN_CONV - 1):, :]
    else:
        hist = _tail_rows(gp, n_seq, seq, FFN_CONV - 1)
    return x_new, hist


def _final_norm_kernel(x_ref, g_ref, o_ref):
    o_ref[...] = _rms(x_ref[...], g_ref[...])


def _final_norm(x, g, tm):
    m, d = x.shape
    return pl.pallas_call(
        _final_norm_kernel,
        grid=(m // tm,),
        in_specs=[pl.BlockSpec((tm, d), lambda i: (i, 0)), pl.BlockSpec((1, d), lambda i: (0, 0))],
        out_specs=pl.BlockSpec((tm, d), lambda i: (i, 0)),
        out_shape=jax.ShapeDtypeStruct((m, d), F32),
        compiler_params=_cparams("arbitrary"),
        name="final_norm",
    )(x, g)


def _rot_half_cols(w):
    d, n = w.shape
    w4 = w.reshape(d, n // HEAD_DIM, 2, HEAD_DIM // 2)
    return jnp.stack([-w4[:, :, 1], w4[:, :, 0]], axis=2).reshape(d, n)


def _rope_tables(pos):
    half = HEAD_DIM // 2
    inv = ROPE_THETA ** (-jnp.arange(half, dtype=F32) / half)
    ang = pos.astype(F32)[:, None] * inv[None, :]
    cos = jnp.tile(jnp.cos(ang), (1, 2 * LANES // HEAD_DIM))
    sin = jnp.tile(jnp.sin(ang), (1, 2 * LANES // HEAD_DIM))
    return cos, sin


def _layer_params(l, P):
    w_in = P["w_in"][l]
    splits = np.cumsum([3 * D_MODEL, MIX_WIDTH, MIX_WIDTH, MIX_WIDTH, MIX_WIDTH, SSM_CONV_DIM, N_HEADS])
    w_gates, w_q, w_k, w_v, w_z, w_xbc, w_dt, w_rw = jnp.split(w_in, splits.tolist(), axis=1)
    pad = jnp.zeros((D_MODEL, COL_GATES - COL_DT - N_HEADS), F32)
    w_main = jnp.concatenate([w_rw, w_dt, pad, w_gates, w_xbc, w_z, w_v], axis=1).astype(BF16)
    w_qk = jnp.concatenate([w_q, w_k], axis=1)
    row = lambda a: a.reshape(1, -1)
    zeros_lora = jnp.zeros((DECAY_LORA, MIX_WIDTH), F32)
    head_of = np.arange(MIX_WIDTH) // HEAD_DIM
    return dict(
        norm1_g=row(P["norm1_g"][l]), w_main=w_main, w_qk=w_qk.astype(BF16), w_qk_rot=_rot_half_cols(w_qk).astype(BF16),
        ssm_conv_w=P["ssm_conv_w"][l], ssm_conv_b=row(P["ssm_conv_b"][l]),
        dtb_c=P["ssm_dt_bias"][l].reshape(-1, 1), dtb_r=row(P["ssm_dt_bias"][l]),
        alog_c=P["ssm_A_log"][l].reshape(-1, 1), alog_r=row(P["ssm_A_log"][l]),
        dskip=row(jnp.repeat(P["ssm_D"][l], HEAD_DIM)), ssm_norm_g=row(P["ssm_norm_g"][l]),
        rwkv_mu=row(P["rwkv_mu"][l]), rwkv_w0=row(P["rwkv_w0"][l]),
        w2_pad=jnp.concatenate([P["rwkv_w2"][l], zeros_lora], axis=0).astype(BF16),
        rwkv_a0=row(P["rwkv_a0"][l]),
        a2_pad=jnp.concatenate([zeros_lora, P["rwkv_a2"][l]], axis=0).astype(BF16),
        rwkv_g2=P["rwkv_g2"][l].astype(BF16),
        rwkv_k_k=row(P["rwkv_k_k"][l]), rwkv_k_a=row(P["rwkv_k_a"][l]), rwkv_r_k=row(P["rwkv_r_k"][l]),
        rwkv_ln_g=row(P["rwkv_ln_g"][l]), rwkv_ln_b=row(P["rwkv_ln_b"][l]),
        head_seg=jnp.asarray(head_of[:, None] == head_of[None, :], BF16),
        w_br=jnp.stack([P["w_br_att"][l], P["w_br_ssm"][l], P["w_br_rwkv"][l]]).astype(BF16),
        w_out=P["w_out"][l].astype(BF16), norm2_g=row(P["norm2_g"][l]), w_up=P["w_up"][l].astype(BF16),
        ffn_conv_w=P["ffn_conv_w"][l], ffn_conv_b=row(P["ffn_conv_b"][l]), w_down=P["w_down"][l].astype(BF16),
    )


def _tail_rows(a, n_seq, seq, k, c0=0, c1=None):
    return a.reshape(n_seq, seq, -1)[:, seq - k:, c0:c1]


def _pad_history(h):
    n, k, c = h.shape
    return jnp.concatenate([jnp.zeros((n, SUBLANES - k, c), h.dtype), h], axis=1)


def _pack_pair_lanes(s):
    n, h, r, c = s.shape
    return s.reshape(n, h // 2, 2, r, c).transpose(0, 1, 3, 2, 4).reshape(n, h // 2, r, 2 * c)


def _unpack_pair_lanes(s):
    n, p, r, c2 = s.shape
    return s.reshape(n, p, r, 2, c2 // 2).transpose(0, 1, 3, 2, 4).reshape(n, 2 * p, r, c2 // 2)


def _pack_pair_diag(s):
    n, h, r, c = s.shape
    s = s.reshape(n, h // 2, 2, r, c)
    z = jnp.zeros_like(s[:, :, 0])
    top = jnp.concatenate([s[:, :, 0], z], axis=-1)
    bot = jnp.concatenate([z, s[:, :, 1]], axis=-1)
    return jnp.concatenate([top, bot], axis=-2)


def _unpack_pair_diag(s):
    n, p, r2, c2 = s.shape
    r, c = r2 // 2, c2 // 2
    return jnp.stack([s[:, :, :r, :c], s[:, :, r:, c:]], axis=2).reshape(n, 2 * p, r, c)


def _trunk_layer(x, n_seq, seq, prm, rope_tab, states, attention):
    m = x.shape[0]
    tm = min(512, m)
    fresh = states is None
    main = _norm_matmul(x, prm["norm1_g"], prm["w_main"], tm, MAIN_WIDTH // 4)
    q, k_new, k_bf16 = _rope_proj(x, prm["norm1_g"], prm["w_qk"], prm["w_qk_rot"], rope_tab[0], rope_tab[1],
                                  min(256, m))
    v_new = main[:, COL_V:COL_V + MIX_WIDTH]
    o_att = attention(q, k_new, k_bf16, v_new)

    xbc_tail = _tail_rows(main, n_seq, seq, SSM_CONV - 1, COL_XBC, COL_XBC + SSM_CONV_DIM)
    dt_t = main[:, COL_DT:COL_DT + N_HEADS].reshape(n_seq, seq, N_HEADS).transpose(0, 2, 1)
    if fresh:
        conv0 = jnp.zeros((n_seq, SUBLANES, SSM_CONV_DIM), F32)
        ssm0 = jnp.zeros((n_seq, N_PAIRS, SSM_STATE, LANES), F32)
        shift0 = jnp.zeros((n_seq, SUBLANES, RWKV_SHIFT_DIM), F32)
        rwkv0 = jnp.zeros((n_seq, N_PAIRS, LANES, LANES), F32)
    else:
        conv0 = _pad_history(states["ssm_conv"])
        ssm0 = _pack_pair_lanes(states["ssm"])
        shift0 = _pad_history(states["rwkv_shift"][:, None, :])
        rwkv0 = _pack_pair_diag(states["rwkv"])
    o_ssm, ssm_fin = _ssd(main, dt_t, conv0, ssm0, prm, n_seq, seq, math.gcd(seq, SSM_CHUNK))
    o_rwkv, rwkv_fin = _rwkv(main, shift0, rwkv0, prm, n_seq, seq, min(seq, RWKV_CHUNK))
    x = _merge(o_att, o_ssm, o_rwkv, main, x, prm["w_br"], prm["w_out"], tm)

    if fresh:
        x, ffn_hist = _ffn(x, prm["norm2_g"], prm["w_up"], prm["ffn_conv_w"], prm["ffn_conv_b"], prm["w_down"],
                           n_seq, seq, min(256, seq))
    else:
        hist = states["ffn_conv"]
        p1 = jnp.repeat(hist[:, 1], seq, axis=0)
        p2 = jnp.repeat(hist[:, 0], seq, axis=0)
        x, ffn_hist = _ffn(x, prm["norm2_g"], prm["w_up"], prm["ffn_conv_w"], prm["ffn_conv_b"], prm["w_down"],
                           n_seq, seq, m, prev=(p1, p2))
    new_states = (
        k_new.reshape(n_seq, seq, N_HEADS, HEAD_DIM), v_new.reshape(n_seq, seq, N_HEADS, HEAD_DIM),
        _unpack_pair_lanes(ssm_fin), xbc_tail, _unpack_pair_diag(rwkv_fin),
        _tail_rows(main, n_seq, seq, 1, COL_RW, COL_RW + RWKV_SHIFT_DIM)[:, 0],
        ffn_hist,
    )
    return x, new_states


def _prompt_attention(n_seq, seq):
    nb = seq // MOBA_BLOCK

    def attend(q, k_new, k_bf16, v_new):
        kmean = _kmean(k_new, n_seq, seq)
        bias = _moba_select(q, kmean, n_seq, seq, MOBA_BLOCK)
        k_blocks = k_bf16.reshape(n_seq, nb, MOBA_BLOCK, MIX_WIDTH)
        vt = v_new.astype(BF16).reshape(n_seq, nb, MOBA_BLOCK, N_HEADS, HEAD_DIM).transpose(0, 1, 3, 4, 2)
        extra = jnp.zeros((n_seq, nb, N_HEADS, VT_ROWS - HEAD_DIM, MOBA_BLOCK), BF16).at[:, :, :, 0, :].set(1.0)
        vt_blocks = jnp.concatenate([vt, extra], axis=3).reshape(n_seq, nb, N_HEADS * VT_ROWS, MOBA_BLOCK)
        o_t = _moba_prompt(q, k_blocks, vt_blocks, bias, n_seq, seq)
        return o_t.transpose(0, 2, 1).reshape(n_seq * seq, MIX_WIDTH)

    return attend


def _transposed_pages(cache):
    l, n, t, h, d = cache.shape
    return cache.transpose(0, 1, 3, 4, 2).reshape(l, n, h * d, t)


def _sample_attention(n_seq, seq, page_table, cache_kt, cache_vt, layer):
    head_of = np.arange(MIX_WIDTH) // HEAD_DIM
    blockdiag = jnp.asarray(np.arange(N_HEADS)[:, None] == head_of[None, :], F32)

    def attend(q, k_new, k_bf16, v_new):
        del k_bf16
        q = q.reshape(n_seq, 1, seq, MIX_WIDTH) * (HEAD_DIM ** -0.5)
        qbd = (q * blockdiag[None, :, None, :]).reshape(n_seq, N_HEADS * seq, MIX_WIDTH)
        o = _moba_sample(page_table, qbd, cache_kt, cache_vt, layer,
                         k_new.reshape(n_seq, seq, MIX_WIDTH), v_new.reshape(n_seq, seq, MIX_WIDTH))
        return o.reshape(n_seq * seq, MIX_WIDTH)

    return attend


def kernel(x_prompt, x_sample, cache_k, cache_v, page_table, state_ssm, state_ssm_conv, state_rwkv, state_rwkv_shift, state_ffn_conv, norm1_g, w_in, ssm_conv_w, ssm_conv_b, ssm_dt_bias, ssm_A_log, ssm_D, ssm_norm_g, rwkv_mu, rwkv_w0, rwkv_w2, rwkv_a0, rwkv_a2, rwkv_g2, rwkv_k_k, rwkv_k_a, rwkv_r_k, rwkv_ln_g, rwkv_ln_b, w_br_att, w_br_ssm, w_br_rwkv, w_out, norm2_g, w_up, ffn_conv_w, ffn_conv_b, w_down, final_norm_g):
    P = dict(norm1_g=norm1_g, w_in=w_in, ssm_conv_w=ssm_conv_w, ssm_conv_b=ssm_conv_b, ssm_dt_bias=ssm_dt_bias,
             ssm_A_log=ssm_A_log, ssm_D=ssm_D, ssm_norm_g=ssm_norm_g, rwkv_mu=rwkv_mu, rwkv_w0=rwkv_w0,
             rwkv_w2=rwkv_w2, rwkv_a0=rwkv_a0, rwkv_a2=rwkv_a2, rwkv_g2=rwkv_g2, rwkv_k_k=rwkv_k_k,
             rwkv_k_a=rwkv_k_a, rwkv_r_k=rwkv_r_k, rwkv_ln_g=rwkv_ln_g, rwkv_ln_b=rwkv_ln_b, w_br_att=w_br_att,
             w_br_ssm=w_br_ssm, w_br_rwkv=w_br_rwkv, w_out=w_out, norm2_g=norm2_g, w_up=w_up,
             ffn_conv_w=ffn_conv_w, ffn_conv_b=ffn_conv_b, w_down=w_down)
    b_p, t_p, d = x_prompt.shape
    b_s, t_s, _ = x_sample.shape
    n_pages = page_table.shape[1]
    past_len = n_pages * cache_k.shape[2]
    depth = w_in.shape[0]
    xp = x_prompt.reshape(b_p * t_p, d)
    xs = x_sample.reshape(b_s * t_s, d)
    rope_p = _rope_tables(jnp.arange(t_p))
    rope_s = _rope_tables(jnp.tile(past_len + jnp.arange(t_s), b_s))
    att_p = _prompt_attention(b_p, t_p)
    cache_kt = _transposed_pages(cache_k)
    cache_vt = _transposed_pages(cache_v)
    new_p, new_s = [], []
    for l in range(depth):
        prm = _layer_params(l, P)
        xp, st_p = _trunk_layer(xp, b_p, t_p, prm, rope_p, None, att_p)
        states = dict(ssm=state_ssm[l], ssm_conv=state_ssm_conv[l], rwkv=state_rwkv[l],
                      rwkv_shift=state_rwkv_shift[l], ffn_conv=state_ffn_conv[l])
        att_s = _sample_attention(b_s, t_s, page_table, cache_kt, cache_vt, l)
        xs, st_s = _trunk_layer(xs, b_s, t_s, prm, rope_s, states, att_s)
        new_p.append(st_p)
        new_s.append(st_s)
    g_fin = final_norm_g.reshape(1, d)
    y_prompt = _final_norm(xp, g_fin, min(512, xp.shape[0])).reshape(b_p, t_p, d)
    y_sample = _final_norm(xs, g_fin, min(512, xs.shape[0])).reshape(b_s, t_s, d)
    stacked = lambda states, i: jnp.stack([s[i] for s in states])
    outs = [y_prompt, y_sample, stacked(new_p, 0), stacked(new_p, 1), stacked(new_s, 0), stacked(new_s, 1)]
    for i in range(2, 7):
        outs.append(stacked(new_p, i))
        outs.append(stacked(new_s, i))
    return tuple(outs)
```

```python
import functools
import math

import jax
import jax.numpy as jnp
import numpy as np
from jax import lax
from jax.experimental import pallas as pl
from jax.experimental.pallas import tpu as pltpu

F32 = jnp.float32
BF16 = jnp.bfloat16

D_MODEL = 1024
DEPTH = 4
PAGE_SIZE = 128
MIX_WIDTH = D_MODEL // 2
HEAD_DIM = 64
N_HEADS = MIX_WIDTH // HEAD_DIM
N_PAIRS = N_HEADS // 2
MOBA_BLOCK = 256
MOBA_TOPK = 3
KEY_SUB = 128
SAMPLE_BLOCKS_PER_STEP = 4
VT_ROWS = HEAD_DIM + 16
ROPE_THETA = 10000.0
SSM_GROUPS = 2
SSM_STATE = 128
SSM_CONV = 4
SSM_CHUNK = 128
SSM_CONV_DIM = MIX_WIDTH + 2 * SSM_GROUPS * SSM_STATE
DECAY_LORA = 64
ICLR_LORA = 64
GATE_LORA = 128
RWKV_SHIFT_DIM = 3 * MIX_WIDTH + DECAY_LORA + ICLR_LORA + GATE_LORA
RWKV_GN_EPS = 64e-5
RWKV_CHUNK = 128
D_FF = ((8 * D_MODEL // 3 + 127) // 128) * 128
FFN_CONV = 3
EPS = 1e-6
LANES = 128
SUBLANES = 8
NEG = -1e30
VMEM_LIMIT = 52 * 1024 * 1024

COL_RW = 0
COL_DT = RWKV_SHIFT_DIM
COL_GATES = 2048
COL_XBC = COL_GATES + 3 * D_MODEL
COL_Z = COL_XBC + SSM_CONV_DIM
COL_V = COL_Z + MIX_WIDTH
MAIN_WIDTH = COL_V + MIX_WIDTH


def _cparams(*sem):
    return pltpu.CompilerParams(dimension_semantics=sem, vmem_limit_bytes=VMEM_LIMIT)


def _rms(x, g):
    return x * lax.rsqrt(jnp.mean(x * x, axis=-1, keepdims=True) + EPS) * g


def _dot(a, b):
    return jnp.dot(a.astype(BF16), b.astype(BF16), preferred_element_type=F32)


def _dot_nt(a, b):
    return lax.dot_general(a.astype(BF16), b.astype(BF16), (((1,), (1,)), ((), ())), preferred_element_type=F32)


def _dot_tn(a, b):
    return lax.dot_general(a.astype(BF16), b.astype(BF16), (((0,), (0,)), ((), ())), preferred_element_type=F32)


def _dot_f32(a, b):
    return jnp.dot(a, b, preferred_element_type=F32, precision=lax.Precision.HIGHEST)


def _dot_split(a, b01):
    hi = a.astype(BF16)
    mid = (a - hi.astype(F32)).astype(BF16)
    b = b01.astype(BF16)
    dot = lambda x: jnp.dot(x, b, preferred_element_type=F32)
    return dot(hi) + dot(mid)


def _silu(x):
    return x * jax.nn.sigmoid(x)


def _softplus(x):
    return jnp.maximum(x, 0.0) + jnp.log(1.0 + jnp.exp(-jnp.abs(x)))


def _shift_rows(g, k, prevs, period):
    rows = g.shape[0]
    out = pltpu.roll(g, k, axis=0)
    tig = lax.broadcasted_iota(jnp.int32, (rows, 1), 0) & (period - 1)
    for t0 in range(k):
        out = jnp.where(tig == t0, prevs[k - t0 - 1], out)
    return out


def _norm_matmul_kernel(x_ref, g_ref, w_ref, o_ref):
    o_ref[...] = _dot(_rms(x_ref[...], g_ref[...]), w_ref[...])


def _norm_matmul(x, g, w, tm, tn):
    m, d = x.shape
    n = w.shape[1]
    return pl.pallas_call(
        _norm_matmul_kernel,
        grid=(n // tn, m // tm),
        in_specs=[pl.BlockSpec((tm, d), lambda j, i: (i, 0)),
                  pl.BlockSpec((1, d), lambda j, i: (0, 0)),
                  pl.BlockSpec((d, tn), lambda j, i: (0, j))],
        out_specs=pl.BlockSpec((tm, tn), lambda j, i: (i, j)),
        out_shape=jax.ShapeDtypeStruct((m, n), F32),
        compiler_params=_cparams("arbitrary", "arbitrary"),
        name="norm_matmul",
    )(x, g, w)


def _rope_proj_kernel(x_ref, g_ref, w_ref, wr_ref, cos_ref, sin_ref, q_ref, k_ref, kb_ref):
    h = _rms(x_ref[...], g_ref[...]).astype(BF16)
    a = jnp.dot(h, w_ref[...], preferred_element_type=F32)
    b = jnp.dot(h, wr_ref[...], preferred_element_type=F32)
    reps = a.shape[1] // LANES
    cos = jnp.concatenate([cos_ref[...]] * reps, axis=1)
    sin = jnp.concatenate([sin_ref[...]] * reps, axis=1)
    qk = a * cos + b * sin
    half = qk.shape[1] // 2
    q_ref[...] = qk[:, :half]
    k_ref[...] = qk[:, half:]
    kb_ref[...] = qk[:, half:].astype(BF16)


def _rope_proj(x, g, w, wr, cos, sin, tm):
    m, d = x.shape
    n = w.shape[1]
    half = n // 2
    tab_blocks = cos.shape[0] // tm
    return pl.pallas_call(
        _rope_proj_kernel,
        grid=(m // tm,),
        in_specs=[pl.BlockSpec((tm, d), lambda i: (i, 0)),
                  pl.BlockSpec((1, d), lambda i: (0, 0)),
                  pl.BlockSpec((d, n), lambda i: (0, 0)),
                  pl.BlockSpec((d, n), lambda i: (0, 0)),
                  pl.BlockSpec((tm, LANES), lambda i: (i % tab_blocks, 0)),
                  pl.BlockSpec((tm, LANES), lambda i: (i % tab_blocks, 0))],
        out_specs=[pl.BlockSpec((tm, half), lambda i: (i, 0))] * 3,
        out_shape=[jax.ShapeDtypeStruct((m, half), F32), jax.ShapeDtypeStruct((m, half), F32),
                   jax.ShapeDtypeStruct((m, half), BF16)],
        compiler_params=_cparams("arbitrary"),
        name="rope_proj",
    )(x, g, w, wr, cos, sin)


def _kmean_kernel(k_ref, o_ref):
    k = k_ref[...]
    o_ref[...] = jnp.mean(k.reshape(SUBLANES, MOBA_BLOCK, k.shape[-1]), axis=1)


def _kmean(k2d, n_seq, seq):
    nb = seq // MOBA_BLOCK
    w = k2d.shape[1]
    rows = SUBLANES * MOBA_BLOCK
    return pl.pallas_call(
        _kmean_kernel,
        grid=(n_seq * nb // SUBLANES,),
        in_specs=[pl.BlockSpec((rows, w), lambda i: (i, 0))],
        out_specs=pl.BlockSpec((SUBLANES, w), lambda i: (i, 0)),
        out_shape=jax.ShapeDtypeStruct((n_seq * nb, w), F32),
        compiler_params=_cparams("arbitrary"),
        name="moba_kmean",
    )(k2d).reshape(n_seq, nb, w)


def _head_mask(pair_vals, hl):
    lane = lax.broadcasted_iota(jnp.int32, pair_vals.shape, 1)
    return jnp.where((lane >= hl * HEAD_DIM) & (lane < (hl + 1) * HEAD_DIM), pair_vals, jnp.zeros_like(pair_vals))


def _moba_select_kernel(q_ref, km_ref, o_ref, *, nb):
    tq = q_ref.shape[0]
    q0 = pl.program_id(1) * tq
    qpos = q0 + lax.broadcasted_iota(jnp.int32, (1, tq), 1)
    own = qpos // MOBA_BLOCK
    kb = lax.broadcasted_iota(jnp.int32, (nb, tq), 0)
    kbf = kb.astype(F32)
    valid = kb < own
    for h in range(N_HEADS):
        p, hl = divmod(h, 2)
        lanes = slice(p * LANES, (p + 1) * LANES)
        km = _head_mask(km_ref[:, lanes], hl)
        gate = lax.dot_general(km, q_ref[:, lanes], (((1,), (1,)), ((), ())),
                               preferred_element_type=F32, precision=lax.Precision.HIGHEST)
        g = jnp.where(valid, gate, -jnp.inf)
        sel = jnp.zeros((nb, tq), jnp.bool_)
        for _ in range(MOBA_TOPK):
            m = jnp.max(g, axis=0, keepdims=True)
            first = jnp.min(jnp.where(g == m, kbf, float(nb)), axis=0, keepdims=True)
            hit = (kbf == first) & (m > -jnp.inf)
            sel = sel | hit
            g = jnp.where(hit, -jnp.inf, g)
        o_ref[h] = jnp.where(sel, 0.0, NEG)


def _moba_select(q2d, kmean, n_seq, seq, tq):
    nb = seq // MOBA_BLOCK
    w = q2d.shape[1]
    tiles = seq // tq
    return pl.pallas_call(
        functools.partial(_moba_select_kernel, nb=nb),
        grid=(n_seq, tiles),
        in_specs=[pl.BlockSpec((tq, w), lambda b, i: (b * tiles + i, 0)),
                  pl.BlockSpec((None, nb, w), lambda b, i: (b, 0, 0))],
        out_specs=pl.BlockSpec((None, N_HEADS, nb, tq), lambda b, i: (b, 0, 0, i)),
        out_shape=jax.ShapeDtypeStruct((n_seq, N_HEADS, nb, seq), F32),
        compiler_params=_cparams("arbitrary", "arbitrary"),
        name="moba_select",
    )(q2d, kmean)


def _moba_prompt_kernel(q_ref, k_ref, vt_ref, bias_ref, o_ref, qm_scr, m_scr, acc_scr, s_scr):
    tq = q_ref.shape[0]
    j = pl.program_id(1)
    scale = HEAD_DIM ** -0.5 * math.log2(math.e)
    key_i = lax.broadcasted_iota(jnp.int32, (MOBA_BLOCK, tq), 0)
    qry_i = lax.broadcasted_iota(jnp.int32, (MOBA_BLOCK, tq), 1)
    causal = key_i <= qry_i
    heads = range(N_HEADS)
    pair = [slice((h // 2) * LANES, (h // 2 + 1) * LANES) for h in heads]
    vrow = [slice(h * VT_ROWS, (h + 1) * VT_ROWS) for h in heads]
    qm = [_head_mask((q_ref[:, pair[h]] * scale).astype(BF16), h % 2) for h in heads]
    for h in heads:
        qm_scr[h] = qm[h]
    s_own = [jnp.where(causal, _dot_nt(k_ref[j, :, pair[h]], qm[h]), NEG) for h in heads]
    m_own = [jnp.max(s_own[h], axis=0, keepdims=True) for h in heads]
    for h in heads:
        m_scr[h:h + 1, :] = m_own[h]
        acc_scr[vrow[h], :] = _dot(vt_ref[j, vrow[h], :], jnp.exp2(s_own[h] - m_own[h]))
        s_scr[0, h] = _dot_nt(k_ref[0, 0:KEY_SUB, pair[h]], qm[h])

    def body(kb, carry):
        for sub in range(MOBA_BLOCK // KEY_SUB):
            cur = sub % 2
            if sub + 1 < MOBA_BLOCK // KEY_SUB:
                kn, r0 = kb, (sub + 1) * KEY_SUB
            else:
                kn, r0 = jnp.minimum(kb + 1, j), 0
            for h in range(N_HEADS):
                lanes = slice((h // 2) * LANES, (h // 2 + 1) * LANES)
                s_scr[1 - cur, h] = _dot_nt(k_ref[kn, r0:r0 + KEY_SUB, lanes], qm_scr[h])
            for h in range(N_HEADS):
                rows = slice(h * VT_ROWS, (h + 1) * VT_ROWS)
                s = s_scr[cur, h] + bias_ref[h, pl.ds(kb, 1), :]
                m = m_scr[h:h + 1, :]
                m_new = jnp.maximum(m, jnp.max(s, axis=0, keepdims=True))
                m_scr[h:h + 1, :] = m_new
                acc_scr[rows, :] = (acc_scr[rows, :] * jnp.exp2(m - m_new)
                                    + _dot(vt_ref[kb, rows, sub * KEY_SUB:(sub + 1) * KEY_SUB], jnp.exp2(s - m_new)))
        return carry

    lax.fori_loop(0, j, body, 0)
    for h in range(N_HEADS):
        r0 = h * VT_ROWS
        o_ref[h * HEAD_DIM:(h + 1) * HEAD_DIM, :] = (acc_scr[r0:r0 + HEAD_DIM, :]
                                                     / acc_scr[r0 + HEAD_DIM:r0 + HEAD_DIM + 1, :])


def _moba_prompt(q2d, k_blocks, vt_blocks, bias, n_seq, seq):
    nb = seq // MOBA_BLOCK
    w = MIX_WIDTH
    tq = MOBA_BLOCK
    return pl.pallas_call(
        _moba_prompt_kernel,
        grid=(n_seq, nb),
        in_specs=[pl.BlockSpec((tq, w), lambda b, i: (b * nb + i, 0)),
                  pl.BlockSpec((None, nb, MOBA_BLOCK, w), lambda b, i: (b, 0, 0, 0)),
                  pl.BlockSpec((None, nb, N_HEADS * VT_ROWS, MOBA_BLOCK), lambda b, i: (b, 0, 0, 0)),
                  pl.BlockSpec((None, N_HEADS, nb, tq), lambda b, i: (b, 0, 0, i))],
        out_specs=pl.BlockSpec((None, w, tq), lambda b, i: (b, 0, i)),
        out_shape=jax.ShapeDtypeStruct((n_seq, w, seq), F32),
        scratch_shapes=[pltpu.VMEM((N_HEADS, tq, LANES), BF16), pltpu.VMEM((N_HEADS, tq), F32),
                        pltpu.VMEM((N_HEADS * VT_ROWS, tq), F32),
                        pltpu.VMEM((2, N_HEADS, KEY_SUB, tq), F32)],
        compiler_params=_cparams("arbitrary", "arbitrary"),
        name="moba_prompt",
    )(q2d, k_blocks, vt_blocks, bias)


def _moba_sample_kernel(pt_ref, qbd_ref, *refs, nb, blocks_per_step):
    del pt_ref
    n_pages = 2 * blocks_per_step
    k_refs = refs[:n_pages]
    v_refs = refs[n_pages:2 * n_pages]
    kn_ref, vn_ref, o_ref, r_scr, m_scr, l_scr, g_scr = refs[2 * n_pages:]
    step = pl.program_id(1)
    nq = qbd_ref.shape[0]
    lane = lax.broadcasted_iota(jnp.int32, (nq, LANES), 1)

    @pl.when(step == 0)
    def _():
        m_scr[...] = jnp.full(m_scr.shape, NEG, F32)
        l_scr[...] = jnp.zeros(l_scr.shape, F32)
        g_scr[...] = jnp.full(g_scr.shape, -jnp.inf, F32)

    qbd = qbd_ref[...].astype(BF16)
    m_all, l_all, g_all = m_scr[...], l_scr[...], g_scr[...]
    blocks = range(blocks_per_step)
    s = [_dot(qbd, jnp.concatenate([k_refs[2 * i][...], k_refs[2 * i + 1][...]], axis=1)) for i in blocks]
    m_b = [jnp.max(s[i], axis=-1, keepdims=True) for i in blocks]
    p = [jnp.exp(s[i] - m_b[i]) for i in blocks]
    for i in blocks:
        b = step * blocks_per_step + i
        vt = jnp.concatenate([v_refs[2 * i][...], v_refs[2 * i + 1][...]], axis=1)
        r_scr[b] = _dot_nt(p[i], vt)
        col = lane == b
        m_all = jnp.where(col, m_b[i], m_all)
        l_all = jnp.where(col, jnp.sum(p[i], axis=-1, keepdims=True), l_all)
        g_all = jnp.where(col, jnp.sum(s[i], axis=-1, keepdims=True), g_all)
    m_scr[...] = m_all
    l_scr[...] = l_all
    g_scr[...] = g_all

    @pl.when(step == nb // blocks_per_step - 1)
    def _():
        lane_f = lane.astype(F32)
        g = g_scr[...]
        sel = jnp.zeros((nq, LANES), jnp.bool_)
        for _ in range(MOBA_TOPK):
            mx = jnp.max(g, axis=-1, keepdims=True)
            first = jnp.min(jnp.where(g == mx, lane_f, float(LANES)), axis=-1, keepdims=True)
            hit = (lane_f == first) & (mx > -jnp.inf)
            sel = sel | hit
            g = jnp.where(hit, -jnp.inf, g)
        n_new = kn_ref.shape[0]
        s_own = _dot_nt(qbd, kn_ref[...])
        q_t = lax.broadcasted_iota(jnp.int32, (nq, n_new), 0) & (n_new - 1)
        k_t = lax.broadcasted_iota(jnp.int32, (nq, n_new), 1)
        s_own = jnp.where(k_t <= q_t, s_own, NEG)
        m_all = m_scr[...]
        m_tot = jnp.maximum(jnp.max(jnp.where(sel, m_all, NEG), axis=-1, keepdims=True),
                            jnp.max(s_own, axis=-1, keepdims=True))
        w_sel = jnp.where(sel, jnp.exp(m_all - m_tot), 0.0)
        p_own = jnp.exp(s_own - m_tot)
        l_tot = jnp.sum(w_sel * l_scr[...], axis=-1, keepdims=True) + jnp.sum(p_own, axis=-1, keepdims=True)
        acc = _dot(p_own, vn_ref[...])
        for bb in range(nb):
            acc = acc + w_sel[:, bb:bb + 1] * r_scr[bb]
        acc = acc / l_tot
        lane_w = lax.broadcasted_iota(jnp.int32, (n_new, acc.shape[1]), 1) // HEAD_DIM
        out = jnp.zeros((n_new, acc.shape[1]), F32)
        for h in range(N_HEADS):
            out = out + jnp.where(lane_w == h, acc[h * n_new:(h + 1) * n_new, :], 0.0)
        o_ref[...] = out


def _moba_sample(page_table, qbd, cache_kt, cache_vt, layer, k_new, v_new):
    n_seq, nq, w = qbd.shape
    t_new = k_new.shape[1]
    n_pages = page_table.shape[1]
    nb = n_pages * PAGE_SIZE // MOBA_BLOCK
    bps = SAMPLE_BLOCKS_PER_STEP
    assert MOBA_BLOCK == 2 * PAGE_SIZE and nb <= LANES and nb % bps == 0
    page = lambda off: pl.BlockSpec((None, None, w, PAGE_SIZE),
                                    lambda s, b, pt: (layer, pt[s * n_pages + 2 * bps * b + off], 0, 0))
    pages = [page(off) for off in range(2 * bps)]
    grid_spec = pltpu.PrefetchScalarGridSpec(
        num_scalar_prefetch=1,
        grid=(n_seq, nb // bps),
        in_specs=[pl.BlockSpec((None, nq, w), lambda s, b, pt: (s, 0, 0))] + pages + pages
                 + [pl.BlockSpec((None, t_new, w), lambda s, b, pt: (s, 0, 0)),
                    pl.BlockSpec((None, t_new, w), lambda s, b, pt: (s, 0, 0))],
        out_specs=pl.BlockSpec((None, t_new, w), lambda s, b, pt: (s, 0, 0)),
        scratch_shapes=[pltpu.VMEM((nb, nq, w), F32), pltpu.VMEM((nq, LANES), F32),
                        pltpu.VMEM((nq, LANES), F32), pltpu.VMEM((nq, LANES), F32)],
    )
    return pl.pallas_call(
        functools.partial(_moba_sample_kernel, nb=nb, blocks_per_step=bps),
        grid_spec=grid_spec,
        out_shape=jax.ShapeDtypeStruct((n_seq, t_new, w), F32),
        compiler_params=_cparams("arbitrary", "arbitrary"),
        name="moba_sample",
    )(page_table.reshape(-1), qbd, *([cache_kt] * (2 * bps)), *([cache_vt] * (2 * bps)), k_new, v_new)


def _ssd_kernel(xbc_ref, z_ref, dtc_ref, dtt_ref, conv0_ref, s0_ref, cw_ref, cb_ref, dtb_c_ref, dtb_r_ref,
                a_c_ref, a_r_ref, dskip_ref, ng_ref, y_ref, sfin_ref, carry_scr, s_scr):
    c = pl.program_id(1)
    q = xbc_ref.shape[0]

    @pl.when(c == 0)
    def _():
        carry_scr[...] = conv0_ref[...]
        s_scr[...] = s0_ref[...]

    xbc = xbc_ref[...]
    carry = carry_scr[...]
    prevs = [carry[SUBLANES - i:SUBLANES - i + 1, :] for i in range(1, SSM_CONV)]
    conv = cb_ref[...] + cw_ref[SSM_CONV - 1:SSM_CONV, :] * xbc
    for k in range(1, SSM_CONV):
        conv = conv + cw_ref[SSM_CONV - 1 - k:SSM_CONV - k, :] * _shift_rows(xbc, k, prevs, q)
    carry_scr[...] = xbc[q - SUBLANES:, :]
    conv = _silu(conv)
    xs = conv[:, :MIX_WIDTH]
    gw = SSM_STATE
    bmat = [conv[:, MIX_WIDTH + g * gw:MIX_WIDTH + (g + 1) * gw] for g in range(SSM_GROUPS)]
    cmat = [conv[:, MIX_WIDTH + (SSM_GROUPS + g) * gw:MIX_WIDTH + (SSM_GROUPS + g + 1) * gw] for g in range(SSM_GROUPS)]

    dt_c = _softplus(dtc_ref[:, :N_HEADS] + dtb_r_ref[...])
    dt_t = _softplus(dtt_ref[...] + dtb_c_ref[...])
    a_r = -jnp.exp(a_r_ref[...])
    a_c = -jnp.exp(a_c_ref[...])
    ri = lax.broadcasted_iota(jnp.int32, (q, q), 0)
    ci = lax.broadcasted_iota(jnp.int32, (q, q), 1)
    lower = (ri >= ci).astype(F32)
    cum_c = _dot_f32(lower, dt_c * a_r)
    cum_t = _dot_f32(dt_t * a_c, (ri <= ci).astype(F32))
    causal = ri >= ci
    cb = [_dot_nt(cmat[g], bmat[g]) for g in range(SSM_GROUPS)]
    heads_per_group = N_HEADS // SSM_GROUPS
    lane = lax.broadcasted_iota(jnp.int32, (1, LANES), 1)
    first_half = lane < HEAD_DIM

    for p in range(N_PAIRS):
        lanes = slice(p * LANES, (p + 1) * LANES)
        xs_p = xs[:, lanes]
        s_in = s_scr[p]
        y_h, sc_h, cd_h = [], [], []
        for hl in range(2):
            h = 2 * p + hl
            g = h // heads_per_group
            cc = cum_c[:, h:h + 1]
            ct = cum_t[h:h + 1, :]
            lmat = jnp.where(causal, jnp.exp(jnp.where(causal, cc - ct, 0.0)), 0.0)
            y_diag = _dot(cb[g] * lmat * dt_t[h:h + 1, :], xs_p)
            y_off = _dot(cmat[g] * jnp.exp(cc), s_in)
            y_h.append(y_diag + y_off)
            dec_end = jnp.exp(cc[q - 1:q, :] - cc)
            sc_h.append(_dot_tn(bmat[g] * (dec_end * dt_c[:, h:h + 1]), xs_p))
            cd_h.append(jnp.exp(cc[q - 1:q, :]))
        y_ref[:, lanes] = jnp.where(first_half, y_h[0], y_h[1])
        s_scr[p] = (s_in * jnp.where(first_half, cd_h[0], cd_h[1])
                    + jnp.where(first_half, sc_h[0], sc_h[1]))

    y = (y_ref[...] + dskip_ref[...] * xs) * _silu(z_ref[...])
    gwid = MIX_WIDTH // SSM_GROUPS
    for g in range(SSM_GROUPS):
        cols = slice(g * gwid, (g + 1) * gwid)
        y_ref[:, cols] = _rms(y[:, cols], ng_ref[:, cols])
    sfin_ref[...] = s_scr[...]


def _ssd(main, dt_t, conv0, s0, prm, n_seq, seq, chunk):
    nc = seq // chunk
    xw = SSM_CONV_DIM
    row = lambda b, c: b * nc + c
    const2 = lambda b, c: (0, 0)
    return pl.pallas_call(
        _ssd_kernel,
        grid=(n_seq, nc),
        in_specs=[pl.BlockSpec((chunk, xw), lambda b, c: (row(b, c), COL_XBC // xw)),
                  pl.BlockSpec((chunk, MIX_WIDTH), lambda b, c: (row(b, c), COL_Z // MIX_WIDTH)),
                  pl.BlockSpec((chunk, LANES), lambda b, c: (row(b, c), COL_DT // LANES)),
                  pl.BlockSpec((None, N_HEADS, chunk), lambda b, c: (b, 0, c)),
                  pl.BlockSpec((None, SUBLANES, xw), lambda b, c: (b, 0, 0)),
                  pl.BlockSpec((None, N_PAIRS, SSM_STATE, LANES), lambda b, c: (b, 0, 0, 0)),
                  pl.BlockSpec((SSM_CONV, xw), const2),
                  pl.BlockSpec((1, xw), const2),
                  pl.BlockSpec((N_HEADS, 1), const2),
                  pl.BlockSpec((1, N_HEADS), const2),
                  pl.BlockSpec((N_HEADS, 1), const2),
                  pl.BlockSpec((1, N_HEADS), const2),
                  pl.BlockSpec((1, MIX_WIDTH), const2),
                  pl.BlockSpec((1, MIX_WIDTH), const2)],
        out_specs=[pl.BlockSpec((chunk, MIX_WIDTH), lambda b, c: (row(b, c), 0)),
                   pl.BlockSpec((None, N_PAIRS, SSM_STATE, LANES), lambda b, c: (b, 0, 0, 0))],
        out_shape=[jax.ShapeDtypeStruct((n_seq * seq, MIX_WIDTH), F32),
                   jax.ShapeDtypeStruct((n_seq, N_PAIRS, SSM_STATE, LANES), F32)],
        scratch_shapes=[pltpu.VMEM((SUBLANES, xw), F32), pltpu.VMEM((N_PAIRS, SSM_STATE, LANES), F32)],
        compiler_params=_cparams("arbitrary", "arbitrary"),
        name="ssd",
    )(main, main, main, dt_t, conv0, s0, prm["ssm_conv_w"], prm["ssm_conv_b"], prm["dtb_c"], prm["dtb_r"],
      prm["alog_c"], prm["alog_r"], prm["dskip"], prm["ssm_norm_g"])


def _rwkv_kernel(rw_ref, shift0_ref, s0_ref, mu_ref, w0_ref, w2_ref, a0_ref, a2_ref, g2_ref, kk_ref, ka_ref,
                 rk_ref, lng_ref, lnb_ref, seg_ref, y_ref, sfin_ref, carry_scr, s_scr):
    c = pl.program_id(1)
    n_t = rw_ref.shape[0]
    w = MIX_WIDTH

    @pl.when(c == 0)
    def _():
        carry_scr[...] = shift0_ref[...]
        s_scr[...] = s0_ref[...]

    p_in = rw_ref[...]
    p_prev = _shift_rows(p_in, 1, [carry_scr[SUBLANES - 1:SUBLANES, :]], n_t)
    carry_scr[...] = p_in[n_t - SUBLANES:, :]
    xs = p_in + (p_prev - p_in) * mu_ref[...]
    r = xs[:, 0:w]
    k = xs[:, w:2 * w]
    v = xs[:, 2 * w:3 * w]
    lora_in = xs[:, 3 * w:3 * w + LANES]
    gate_in = xs[:, 3 * w + LANES:]
    w_log = -_softplus(-(w0_ref[...] + _dot(jnp.tanh(lora_in), w2_ref[...]))) - 0.5
    logw = -jnp.exp(w_log)
    a = jax.nn.sigmoid(a0_ref[...] + _dot(lora_in, a2_ref[...]))
    g = _dot(jax.nn.sigmoid(gate_in), g2_ref[...])
    seg = seg_ref[...]
    kk = k * kk_ref[...]
    kk = kk * lax.rsqrt(jnp.maximum(_dot_split(kk * kk, seg), 1e-12))
    k = k * (1.0 + (a - 1.0) * ka_ref[...])
    bonus = _dot_split(r * k * rk_ref[...], seg) * v

    ri = lax.broadcasted_iota(jnp.int32, (n_t, n_t), 0)
    ci = lax.broadcasted_iota(jnp.int32, (n_t, n_t), 1)
    strict = ri > ci
    incl = ri >= ci
    cl = _dot_f32(incl.astype(F32), logw)
    mid = n_t // 2 - 1
    cl_mid = cl[mid:mid + 1, :]
    cm = cl - cl_mid
    p_inv = jnp.exp(-cm)
    a_t = -kk * jnp.exp(cm - logw)
    b_t = kk * a * p_inv
    k_t = k * p_inv
    r_t = r * jnp.exp(cm)
    to_mid = jnp.exp(cl_mid)
    mid_to_end = jnp.exp(cm[n_t - 1:n_t, :])
    p_end = jnp.exp(cl[n_t - 1:n_t, :])
    n_dbl = max(int(math.log2(n_t)) - 1, 0)
    first_half = lax.broadcasted_iota(jnp.int32, (1, LANES), 1) < HEAD_DIM
    heads = range(N_HEADS)
    pair_lanes = [slice((h // 2) * LANES, (h // 2 + 1) * LANES) for h in heads]
    am = [_head_mask(a_t[:, pair_lanes[h]], h % 2) for h in heads]
    rm = [_head_mask(r_t[:, pair_lanes[h]], h % 2) for h in heads]
    bm = [_head_mask(b_t[:, pair_lanes[h]], h % 2) for h in heads]
    km = [_head_mask(k_t[:, pair_lanes[h]], h % 2) for h in heads]
    vp = [v[:, pair_lanes[h]] for h in heads]
    gram = [_dot_nt(jnp.concatenate([am[h], rm[h]], axis=0), jnp.concatenate([bm[h], km[h]], axis=0))
            for h in heads]
    l_pow = [jnp.where(strict, gram[h][:n_t, :n_t], 0.0) for h in heads]
    l_ak = [jnp.where(strict, gram[h][:n_t, n_t:], 0.0) for h in heads]
    m_rb = [jnp.where(incl, gram[h][n_t:, :n_t], 0.0) for h in heads]
    m_rk = [jnp.where(incl, gram[h][n_t:, n_t:], 0.0) for h in heads]
    wu = [jnp.concatenate([am[h], _dot(l_ak[h], vp[h])], axis=1) for h in heads]
    wu = [wu[h] + _dot(l_pow[h], wu[h]) for h in heads]
    for _ in range(n_dbl):
        l_pow = [_dot(l_pow[h], l_pow[h]) for h in heads]
        wu = [wu[h] + _dot(l_pow[h], wu[h]) for h in heads]
    mw = [_dot(m_rb[h], wu[h]) for h in heads]
    mv = [_dot(m_rk[h], vp[h]) for h in heads]
    bw = [_dot_tn(wu[h], bm[h]) for h in heads]
    top = lax.broadcasted_iota(jnp.int32, (LANES, LANES), 0) < HEAD_DIM
    diag = top == (lax.broadcasted_iota(jnp.int32, (LANES, LANES), 1) < HEAD_DIM)
    for p in range(N_PAIRS):
        h0, h1 = 2 * p, 2 * p + 1
        lanes = pair_lanes[h0]
        q_hat = rm[h0] + mw[h0][:, :LANES] + rm[h1] + mw[h1][:, :LANES]
        y_hat = jnp.where(first_half, mw[h0][:, LANES:] + mv[h0], mw[h1][:, LANES:] + mv[h1])
        s_in = s_scr[p]
        s_mid = s_in * to_mid[:, lanes]
        y_ref[:, lanes] = _dot_nt(q_hat, s_mid) + y_hat
        vk = _dot_tn(vp[h0], k_t[:, lanes])
        h_t = jnp.where(diag, jnp.where(top, bw[h0][LANES:, :], bw[h1][LANES:, :]) + vk, 0.0)
        wb = bw[h0][:LANES, :] + bw[h1][:LANES, :]
        s_scr[p] = s_in * p_end[:, lanes] + (_dot(s_mid, wb) + h_t) * mid_to_end[:, lanes]

    y = y_ref[...]
    mean = _dot_split(y, seg) * (1.0 / HEAD_DIM)
    yc = y - mean
    var = _dot_split(yc * yc, seg) * (1.0 / HEAD_DIM)
    yn = yc * lax.rsqrt(var + RWKV_GN_EPS) * lng_ref[...] + lnb_ref[...]
    y_ref[...] = (yn + bonus) * g
    sfin_ref[...] = s_scr[...]


def _rwkv(main, shift0, s0, prm, n_seq, seq, chunk):
    nc = seq // chunk
    rw = RWKV_SHIFT_DIM
    w = MIX_WIDTH
    const2 = lambda b, c: (0, 0)
    vec = pl.BlockSpec((1, w), const2)
    return pl.pallas_call(
        _rwkv_kernel,
        grid=(n_seq, nc),
        in_specs=[pl.BlockSpec((chunk, rw), lambda b, c: (b * nc + c, COL_RW // rw)),
                  pl.BlockSpec((None, SUBLANES, rw), lambda b, c: (b, 0, 0)),
                  pl.BlockSpec((None, N_PAIRS, LANES, LANES), lambda b, c: (b, 0, 0, 0)),
                  pl.BlockSpec((1, rw), const2),
                  vec, pl.BlockSpec((LANES, w), const2),
                  vec, pl.BlockSpec((LANES, w), const2),
                  pl.BlockSpec((GATE_LORA, w), const2),
                  vec, vec, vec, vec, vec,
                  pl.BlockSpec((w, w), const2)],
        out_specs=[pl.BlockSpec((chunk, w), lambda b, c: (b * nc + c, 0)),
                   pl.BlockSpec((None, N_PAIRS, LANES, LANES), lambda b, c: (b, 0, 0, 0))],
        out_shape=[jax.ShapeDtypeStruct((n_seq * seq, w), F32),
                   jax.ShapeDtypeStruct((n_seq, N_PAIRS, LANES, LANES), F32)],
        scratch_shapes=[pltpu.VMEM((SUBLANES, rw), F32), pltpu.VMEM((N_PAIRS, LANES, LANES), F32)],
        compiler_params=_cparams("arbitrary", "arbitrary"),
        name="rwkv7",
    )(main, shift0, s0, prm["rwkv_mu"], prm["rwkv_w0"], prm["w2_pad"], prm["rwkv_a0"], prm["a2_pad"],
      prm["rwkv_g2"], prm["rwkv_k_k"], prm["rwkv_k_a"], prm["rwkv_r_k"], prm["rwkv_ln_g"], prm["rwkv_ln_b"],
      prm["head_seg"])


def _merge_kernel(oa_ref, os_ref, or_ref, ga_ref, gs_ref, gr_ref, x_ref, wbr_ref, wo_ref, o_ref):
    merged = (jax.nn.sigmoid(ga_ref[...]) * _dot(oa_ref[...], wbr_ref[0])
              + jax.nn.sigmoid(gs_ref[...]) * _dot(os_ref[...], wbr_ref[1])
              + jax.nn.sigmoid(gr_ref[...]) * _dot(or_ref[...], wbr_ref[2]))
    o_ref[...] = x_ref[...] + _dot(merged, wo_ref[...])


def _merge(o_att, o_ssm, o_rwkv, main, x, w_br, w_out, tm):
    m, d = x.shape
    w = MIX_WIDTH
    act = pl.BlockSpec((tm, w), lambda i: (i, 0))
    gate = lambda b: pl.BlockSpec((tm, d), lambda i: (i, COL_GATES // d + b))
    return pl.pallas_call(
        _merge_kernel,
        grid=(m // tm,),
        in_specs=[act, act, act, gate(0), gate(1), gate(2),
                  pl.BlockSpec((tm, d), lambda i: (i, 0)),
                  pl.BlockSpec((3, w, d), lambda i: (0, 0, 0)),
                  pl.BlockSpec((d, d), lambda i: (0, 0))],
        out_specs=pl.BlockSpec((tm, d), lambda i: (i, 0)),
        out_shape=jax.ShapeDtypeStruct((m, d), F32),
        compiler_params=_cparams("arbitrary"),
        name="merge_out",
    )(o_att, o_ssm, o_rwkv, main, main, main, x, w_br, w_out)


def _ffn_kernel(x_ref, g_ref, wup_ref, cw_ref, cb_ref, wdn_ref, *rest, period, carried):
    if carried:
        o_ref, gp_ref, carry_scr = rest
    else:
        p1_ref, p2_ref, o_ref, gp_ref = rest
    x = x_ref[...]
    up = _dot(_rms(x, g_ref[...]), wup_ref[...])
    g_pre = up[:, :D_FF]
    u = up[:, D_FF:]
    tm = x.shape[0]
    gp_ref[...] = g_pre[tm - gp_ref.shape[0]:, :]
    if carried:
        @pl.when(pl.program_id(1) == 0)
        def _():
            carry_scr[...] = jnp.zeros(carry_scr.shape, F32)
        prevs = [carry_scr[SUBLANES - i:SUBLANES - i + 1, :] for i in range(1, FFN_CONV)]
    else:
        prevs = [p1_ref[...], p2_ref[...]]
    conv = cb_ref[...] + cw_ref[FFN_CONV - 1:FFN_CONV, :] * g_pre
    for k in range(1, FFN_CONV):
        conv = conv + cw_ref[FFN_CONV - 1 - k:FFN_CONV - k, :] * _shift_rows(g_pre, k, prevs, period)
    if carried:
        carry_scr[...] = g_pre[tm - SUBLANES:, :]
    o_ref[...] = x + _dot(_silu(conv) * u, wdn_ref[...])


def _ffn(x, g, w_up, conv_w, conv_b, w_down, n_seq, seq, tm, prev=None):
    m, d = x.shape
    carried = prev is None
    const = lambda *_: (0, 0)
    if carried:
        tiles = seq // tm
        grid = (n_seq, tiles)
        row = lambda b, i: (b * tiles + i, 0)
        period = tm
        extra_in, extra_args = [], []
        scratch = [pltpu.VMEM((SUBLANES, D_FF), F32)]
        gp_rows = SUBLANES
    else:
        assert m == tm
        grid = (1, 1)
        row = lambda b, i: (0, 0)
        period = seq
        extra_in = [pl.BlockSpec((tm, D_FF), row)] * 2
        extra_args = list(prev)
        scratch = []
        gp_rows = tm
    x_new, gp = pl.pallas_call(
        functools.partial(_ffn_kernel, period=period, carried=carried),
        grid=grid,
        in_specs=[pl.BlockSpec((tm, d), row), pl.BlockSpec((1, d), const),
                  pl.BlockSpec((d, 2 * D_FF), const), pl.BlockSpec((FFN_CONV, D_FF), const),
                  pl.BlockSpec((1, D_FF), const), pl.BlockSpec((D_FF, d), const)] + extra_in,
        out_specs=[pl.BlockSpec((tm, d), row), pl.BlockSpec((gp_rows, D_FF), row)],
        out_shape=[jax.ShapeDtypeStruct((m, d), F32), jax.ShapeDtypeStruct((m // tm * gp_rows, D_FF), F32)],
        scratch_shapes=scratch,
        compiler_params=_cparams("arbitrary", "arbitrary"),
        name="conv_ffn",
    )(x, g, w_up, conv_w, conv_b, w_down, *extra_args)
    if carried:
        hist = gp.reshape(n_seq, seq // tm, SUBLANES, D_FF)[:, -1, SUBLANES - (FFN_CONV - 1):, :]
    else:
        hist = _tail_rows(gp, n_seq, seq, FFN_CONV - 1)
    return x_new, hist


def _final_norm_kernel(x_ref, g_ref, o_ref):
    o_ref[...] = _rms(x_ref[...], g_ref[...])


def _final_norm(x, g, tm):
    m, d = x.shape
    return pl.pallas_call(
        _final_norm_kernel,
        grid=(m // tm,),
        in_specs=[pl.BlockSpec((tm, d), lambda i: (i, 0)), pl.BlockSpec((1, d), lambda i: (0, 0))],
        out_specs=pl.BlockSpec((tm, d), lambda i: (i, 0)),
        out_shape=jax.ShapeDtypeStruct((m, d), F32),
        compiler_params=_cparams("arbitrary"),
        name="final_norm",
    )(x, g)


def _rot_half_cols(w):
    d, n = w.shape
    w4 = w.reshape(d, n // HEAD_DIM, 2, HEAD_DIM // 2)
    return jnp.stack([-w4[:, :, 1], w4[:, :, 0]], axis=2).reshape(d, n)


def _rope_tables(pos):
    half = HEAD_DIM // 2
    inv = ROPE_THETA ** (-jnp.arange(half, dtype=F32) / half)
    ang = pos.astype(F32)[:, None] * inv[None, :]
    cos = jnp.tile(jnp.cos(ang), (1, 2 * LANES // HEAD_DIM))
    sin = jnp.tile(jnp.sin(ang), (1, 2 * LANES // HEAD_DIM))
    return cos, sin


def _layer_params(l, P):
    w_in = P["w_in"][l]
    splits = np.cumsum([3 * D_MODEL, MIX_WIDTH, MIX_WIDTH, MIX_WIDTH, MIX_WIDTH, SSM_CONV_DIM, N_HEADS])
    w_gates, w_q, w_k, w_v, w_z, w_xbc, w_dt, w_rw = jnp.split(w_in, splits.tolist(), axis=1)
    pad = jnp.zeros((D_MODEL, COL_GATES - COL_DT - N_HEADS), F32)
    w_main = jnp.concatenate([w_rw, w_dt, pad, w_gates, w_xbc, w_z, w_v], axis=1).astype(BF16)
    w_qk = jnp.concatenate([w_q, w_k], axis=1)
    row = lambda a: a.reshape(1, -1)
    zeros_lora = jnp.zeros((DECAY_LORA, MIX_WIDTH), F32)
    head_of = np.arange(MIX_WIDTH) // HEAD_DIM
    return dict(
        norm1_g=row(P["norm1_g"][l]), w_main=w_main, w_qk=w_qk.astype(BF16), w_qk_rot=_rot_half_cols(w_qk).astype(BF16),
        ssm_conv_w=P["ssm_conv_w"][l], ssm_conv_b=row(P["ssm_conv_b"][l]),
        dtb_c=P["ssm_dt_bias"][l].reshape(-1, 1), dtb_r=row(P["ssm_dt_bias"][l]),
        alog_c=P["ssm_A_log"][l].reshape(-1, 1), alog_r=row(P["ssm_A_log"][l]),
        dskip=row(jnp.repeat(P["ssm_D"][l], HEAD_DIM)), ssm_norm_g=row(P["ssm_norm_g"][l]),
        rwkv_mu=row(P["rwkv_mu"][l]), rwkv_w0=row(P["rwkv_w0"][l]),
        w2_pad=jnp.concatenate([P["rwkv_w2"][l], zeros_lora], axis=0).astype(BF16),
        rwkv_a0=row(P["rwkv_a0"][l]),
        a2_pad=jnp.concatenate([zeros_lora, P["rwkv_a2"][l]], axis=0).astype(BF16),
        rwkv_g2=P["rwkv_g2"][l].astype(BF16),
        rwkv_k_k=row(P["rwkv_k_k"][l]), rwkv_k_a=row(P["rwkv_k_a"][l]), rwkv_r_k=row(P["rwkv_r_k"][l]),
        rwkv_ln_g=row(P["rwkv_ln_g"][l]), rwkv_ln_b=row(P["rwkv_ln_b"][l]),
        head_seg=jnp.asarray(head_of[:, None] == head_of[None, :], BF16),
        w_br=jnp.stack([P["w_br_att"][l], P["w_br_ssm"][l], P["w_br_rwkv"][l]]).astype(BF16),
        w_out=P["w_out"][l].astype(BF16), norm2_g=row(P["norm2_g"][l]), w_up=P["w_up"][l].astype(BF16),
        ffn_conv_w=P["ffn_conv_w"][l], ffn_conv_b=row(P["ffn_conv_b"][l]), w_down=P["w_down"][l].astype(BF16),
    )


def _tail_rows(a, n_seq, seq, k, c0=0, c1=None):
    return a.reshape(n_seq, seq, -1)[:, seq - k:, c0:c1]


def _pad_history(h):
    n, k, c = h.shape
    return jnp.concatenate([jnp.zeros((n, SUBLANES - k, c), h.dtype), h], axis=1)


def _pack_pair_lanes(s):
    n, h, r, c = s.shape
    return s.reshape(n, h // 2, 2, r, c).transpose(0, 1, 3, 2, 4).reshape(n, h // 2, r, 2 * c)


def _unpack_pair_lanes(s):
    n, p, r, c2 = s.shape
    return s.reshape(n, p, r, 2, c2 // 2).transpose(0, 1, 3, 2, 4).reshape(n, 2 * p, r, c2 // 2)


def _pack_pair_diag(s):
    n, h, r, c = s.shape
    s = s.reshape(n, h // 2, 2, r, c)
    z = jnp.zeros_like(s[:, :, 0])
    top = jnp.concatenate([s[:, :, 0], z], axis=-1)
    bot = jnp.concatenate([z, s[:, :, 1]], axis=-1)
    return jnp.concatenate([top, bot], axis=-2)


def _unpack_pair_diag(s):
    n, p, r2, c2 = s.shape
    r, c = r2 // 2, c2 // 2
    return jnp.stack([s[:, :, :r, :c], s[:, :, r:, c:]], axis=2).reshape(n, 2 * p, r, c)


def _trunk_layer(x, n_seq, seq, prm, rope_tab, states, attention):
    m = x.shape[0]
    tm = min(512, m)
    fresh = states is None
    main = _norm_matmul(x, prm["norm1_g"], prm["w_main"], tm, MAIN_WIDTH // 4)
    q, k_new, k_bf16 = _rope_proj(x, prm["norm1_g"], prm["w_qk"], prm["w_qk_rot"], rope_tab[0], rope_tab[1],
                                  min(256, m))
    v_new = main[:, COL_V:COL_V + MIX_WIDTH]
    o_att = attention(q, k_new, k_bf16, v_new)

    xbc_tail = _tail_rows(main, n_seq, seq, SSM_CONV - 1, COL_XBC, COL_XBC + SSM_CONV_DIM)
    dt_t = main[:, COL_DT:COL_DT + N_HEADS].reshape(n_seq, seq, N_HEADS).transpose(0, 2, 1)
    if fresh:
        conv0 = jnp.zeros((n_seq, SUBLANES, SSM_CONV_DIM), F32)
        ssm0 = jnp.zeros((n_seq, N_PAIRS, SSM_STATE, LANES), F32)
        shift0 = jnp.zeros((n_seq, SUBLANES, RWKV_SHIFT_DIM), F32)
        rwkv0 = jnp.zeros((n_seq, N_PAIRS, LANES, LANES), F32)
    else:
        conv0 = _pad_history(states["ssm_conv"])
        ssm0 = _pack_pair_lanes(states["ssm"])
        shift0 = _pad_history(states["rwkv_shift"][:, None, :])
        rwkv0 = _pack_pair_diag(states["rwkv"])
    o_ssm, ssm_fin = _ssd(main, dt_t, conv0, ssm0, prm, n_seq, seq, math.gcd(seq, SSM_CHUNK))
    o_rwkv, rwkv_fin = _rwkv(main, shift0, rwkv0, prm, n_seq, seq, min(seq, RWKV_CHUNK))
    x = _merge(o_att, o_ssm, o_rwkv, main, x, prm["w_br"], prm["w_out"], tm)

    if fresh:
        x, ffn_hist = _ffn(x, prm["norm2_g"], prm["w_up"], prm["ffn_conv_w"], prm["ffn_conv_b"], prm["w_down"],
                           n_seq, seq, min(256, seq))
    else:
        hist = states["ffn_conv"]
        p1 = jnp.repeat(hist[:, 1], seq, axis=0)
        p2 = jnp.repeat(hist[:, 0], seq, axis=0)
        x, ffn_hist = _ffn(x, prm["norm2_g"], prm["w_up"], prm["ffn_conv_w"], prm["ffn_conv_b"], prm["w_down"],
                           n_seq, seq, m, prev=(p1, p2))
    new_states = (
        k_new.reshape(n_seq, seq, N_HEADS, HEAD_DIM), v_new.reshape(n_seq, seq, N_HEADS, HEAD_DIM),
        _unpack_pair_lanes(ssm_fin), xbc_tail, _unpack_pair_diag(rwkv_fin),
        _tail_rows(main, n_seq, seq, 1, COL_RW, COL_RW + RWKV_SHIFT_DIM)[:, 0],
        ffn_hist,
    )
    return x, new_states


def _prompt_attention(n_seq, seq):
    nb = seq // MOBA_BLOCK

    def attend(q, k_new, k_bf16, v_new):
        kmean = _kmean(k_new, n_seq, seq)
        bias = _moba_select(q, kmean, n_seq, seq, MOBA_BLOCK)
        k_blocks = k_bf16.reshape(n_seq, nb, MOBA_BLOCK, MIX_WIDTH)
        vt = v_new.astype(BF16).reshape(n_seq, nb, MOBA_BLOCK, N_HEADS, HEAD_DIM).transpose(0, 1, 3, 4, 2)
        extra = jnp.zeros((n_seq, nb, N_HEADS, VT_ROWS - HEAD_DIM, MOBA_BLOCK), BF16).at[:, :, :, 0, :].set(1.0)
        vt_blocks = jnp.concatenate([vt, extra], axis=3).reshape(n_seq, nb, N_HEADS * VT_ROWS, MOBA_BLOCK)
        o_t = _moba_prompt(q, k_blocks, vt_blocks, bias, n_seq, seq)
        return o_t.transpose(0, 2, 1).reshape(n_seq * seq, MIX_WIDTH)

    return attend


def _transposed_pages(cache):
    l, n, t, h, d = cache.shape
    return cache.transpose(0, 1, 3, 4, 2).reshape(l, n, h * d, t)


def _sample_attention(n_seq, seq, page_table, cache_kt, cache_vt, layer):
    head_of = np.arange(MIX_WIDTH) // HEAD_DIM
    blockdiag = jnp.asarray(np.arange(N_HEADS)[:, None] == head_of[None, :], F32)

    def attend(q, k_new, k_bf16, v_new):
        del k_bf16
        q = q.reshape(n_seq, 1, seq, MIX_WIDTH) * (HEAD_DIM ** -0.5)
        qbd = (q * blockdiag[None, :, None, :]).reshape(n_seq, N_HEADS * seq, MIX_WIDTH)
        o = _moba_sample(page_table, qbd, cache_kt, cache_vt, layer,
                         k_new.reshape(n_seq, seq, MIX_WIDTH), v_new.reshape(n_seq, seq, MIX_WIDTH))
        return o.reshape(n_seq * seq, MIX_WIDTH)

    return attend


def kernel(x_prompt, x_sample, cache_k, cache_v, page_table, state_ssm, state_ssm_conv, state_rwkv, state_rwkv_shift, state_ffn_conv, norm1_g, w_in, ssm_conv_w, ssm_conv_b, ssm_dt_bias, ssm_A_log, ssm_D, ssm_norm_g, rwkv_mu, rwkv_w0, rwkv_w2, rwkv_a0, rwkv_a2, rwkv_g2, rwkv_k_k, rwkv_k_a, rwkv_r_k, rwkv_ln_g, rwkv_ln_b, w_br_att, w_br_ssm, w_br_rwkv, w_out, norm2_g, w_up, ffn_conv_w, ffn_conv_b, w_down, final_norm_g):
    P = dict(norm1_g=norm1_g, w_in=w_in, ssm_conv_w=ssm_conv_w, ssm_conv_b=ssm_conv_b, ssm_dt_bias=ssm_dt_bias,
             ssm_A_log=ssm_A_log, ssm_D=ssm_D, ssm_norm_g=ssm_norm_g, rwkv_mu=rwkv_mu, rwkv_w0=rwkv_w0,
             rwkv_w2=rwkv_w2, rwkv_a0=rwkv_a0, rwkv_a2=rwkv_a2, rwkv_g2=rwkv_g2, rwkv_k_k=rwkv_k_k,
             rwkv_k_a=rwkv_k_a, rwkv_r_k=rwkv_r_k, rwkv_ln_g=rwkv_ln_g, rwkv_ln_b=rwkv_ln_b, w_br_att=w_br_att,
             w_br_ssm=w_br_ssm, w_br_rwkv=w_br_rwkv, w_out=w_out, norm2_g=norm2_g, w_up=w_up,
             ffn_conv_w=ffn_conv_w, ffn_conv_b=ffn_conv_b, w_down=w_down)
    b_p, t_p, d = x_prompt.shape
    b_s, t_s, _ = x_sample.shape
    n_pages = page_table.shape[1]
    past_len = n_pages * cache_k.shape[2]
    depth = w_in.shape[0]
    xp = x_prompt.reshape(b_p * t_p, d)
    xs = x_sample.reshape(b_s * t_s, d)
    rope_p = _rope_tables(jnp.arange(t_p))
    rope_s = _rope_tables(jnp.tile(past_len + jnp.arange(t_s), b_s))
    att_p = _prompt_attention(b_p, t_p)
    cache_kt = _transposed_pages(cache_k)
    cache_vt = _transposed_pages(cache_v)
    new_p, new_s = [], []
    for l in range(depth):
        prm = _layer_params(l, P)
        xp, st_p = _trunk_layer(xp, b_p, t_p, prm, rope_p, None, att_p)
        states = dict(ssm=state_ssm[l], ssm_conv=state_ssm_conv[l], rwkv=state_rwkv[l],
                      rwkv_shift=state_rwkv_shift[l], ffn_conv=state_ffn_conv[l])
        att_s = _sample_attention(b_s, t_s, page_table, cache_kt, cache_vt, l)
        xs, st_s = _trunk_layer(xs, b_s, t_s, prm, rope_s, states, att_s)
        new_p.append(st_p)
        new_s.append(st_s)
    g_fin = final_norm_g.reshape(1, d)
    y_prompt = _final_norm(xp, g_fin, min(512, xp.shape[0])).reshape(b_p, t_p, d)
    y_sample = _final_norm(xs, g_fin, min(512, xs.shape[0])).reshape(b_s, t_s, d)
    stacked = lambda states, i: jnp.stack([s[i] for s in states])
    outs = [y_prompt, y_sample, stacked(new_p, 0), stacked(new_p, 1), stacked(new_s, 0), stacked(new_s, 1)]
    for i in range(2, 7):
        outs.append(stacked(new_p, i))
        outs.append(stacked(new_s, i))
    return tuple(outs)
```
